```python
import jax, jax.numpy as jnp
from jax import lax
import numpy as np

D_MODEL = 1024
BATCH = 8
SEQ = 2048
DEPTH = 4

RET_HEADS = 4
RET_HEAD_DIM = 128
RET_W = RET_HEADS * RET_HEAD_DIM
RET_CHUNK = 128
CONV_CH = 512
CONV_WIDTH = 31
MOBA_HEADS = 8
MOBA_HEAD_DIM = 64
MOBA_W = MOBA_HEADS * MOBA_HEAD_DIM
MOBA_BLOCK = 256
MOBA_TOPK = 3
MOBA_Q_CHUNK = 64
ROPE_THETA = 10000.0
N_BRANCH = 3
IN_COLS = 4 * RET_W + 2 * CONV_CH + 3 * MOBA_W + N_BRANCH * D_MODEL
FF_DENSE = 2816
N_EXPERTS = 8
TOP_K = 2
FF_EXPERT = 3584
MOE_GROUP = 256
NORM_EPS = 1e-6

kernel_name = 'hybrid_retention_conformer_moba_moe_block'


def rms_norm(x, g):
    x32 = x.astype(jnp.float32)
    y = x32 * lax.rsqrt(jnp.mean(x32 * x32, axis=-1, keepdims=True) + NORM_EPS)
    return (y * g.astype(jnp.float32)).astype(x.dtype)


def layer_norm(x, g, b):
    x32 = x.astype(jnp.float32)
    mu = jnp.mean(x32, axis=-1, keepdims=True)
    var = jnp.mean(jnp.square(x32 - mu), axis=-1, keepdims=True)
    y = (x32 - mu) * lax.rsqrt(var + NORM_EPS)
    return (y * g.astype(jnp.float32) + b.astype(jnp.float32)).astype(x.dtype)


def rope_tables(T, d):
    inv = 1.0 / (ROPE_THETA ** (jnp.arange(0, d, 2, dtype=jnp.float32) / d))
    ang = jnp.arange(T, dtype=jnp.float32)[:, None] * inv[None, :]
    return jnp.cos(ang), jnp.sin(ang)


def apply_rope(x, cos, sin):
    x32 = x.astype(jnp.float32)
    half = x.shape[-1] // 2
    x1, x2 = x32[..., :half], x32[..., half:]
    c = cos[None, :, None, :]
    s = sin[None, :, None, :]
    return jnp.concatenate([x1 * c - x2 * s, x2 * c + x1 * s], axis=-1).astype(x.dtype)


def retention_chunkwise(q, k, v):
    B, T, H, d = q.shape
    C = RET_CHUNK
    nC = T // C
    q, k, v = (t.astype(jnp.float32).reshape(B, nC, C, H, d) for t in (q, k, v))
    log_g = jnp.log(1.0 - jnp.exp2(-5.0 - jnp.arange(H, dtype=jnp.float32)))
    pos = jnp.arange(C, dtype=jnp.float32)
    diff = pos[:, None] - pos[None, :]
    decay_intra = jnp.where(diff[None] >= 0, jnp.exp(diff[None] * log_g[:, None, None]), 0.0)
    scores = jnp.einsum('bcnhd,bcmhd->bchnm', q, k) * decay_intra[None, None]
    y_intra = jnp.einsum('bchnm,bcmhd->bcnhd', scores, v)
    w_state = jnp.exp((C - 1.0 - pos)[None, :] * log_g[:, None])
    kv = jnp.einsum('bcmhd,bcmhe,hm->bchde', k, v, w_state)
    g_chunk = jnp.exp(C * log_g)[None, :, None, None]

    def step(S, kv_c):
        return g_chunk * S + kv_c, S

    _, S_prev = lax.scan(step, jnp.zeros((B, H, d, d), jnp.float32), jnp.moveaxis(kv, 1, 0))
    S_prev = jnp.moveaxis(S_prev, 0, 1)
    w_cross = jnp.exp((pos + 1.0)[:, None] * log_g[None, :])
    y_cross = jnp.einsum('bcnhd,bchde->bcnhe', q, S_prev) * w_cross[None, None, :, :, None]
    return (y_intra + y_cross).reshape(B, T, H, d)


def head_group_norm(y):
    mu = jnp.mean(y, axis=-1, keepdims=True)
    var = jnp.mean(jnp.square(y - mu), axis=-1, keepdims=True)
    return (y - mu) * lax.rsqrt(var + 1e-5)


def conformer_conv(ca, cb, conv_w, conv_b, ln_g, ln_b):
    u = ca * jax.nn.sigmoid(cb)
    y = lax.conv_general_dilated(
        u, conv_w[:, None, :].astype(u.dtype), window_strides=(1,),
        padding=[(CONV_WIDTH - 1, 0)], dimension_numbers=('NWC', 'WIO', 'NWC'),
        feature_group_count=CONV_CH) + conv_b.astype(u.dtype)
    return jax.nn.silu(layer_norm(y, ln_g, ln_b))


def moba_attention(q, k, v):
    B, T, H, d = q.shape
    BS = MOBA_BLOCK
    nB = -(-T // BS)
    pad = nB * BS - T
    q, k, v = (t.transpose(0, 2, 1, 3) for t in (q, k, v))
    kb = jnp.pad(k, ((0, 0), (0, 0), (0, pad), (0, 0))).reshape(B, H, nB, BS, d)
    vb = jnp.pad(v, ((0, 0), (0, 0), (0, pad), (0, 0))).reshape(B, H, nB, BS, d)
    n_in_block = jnp.clip(T - jnp.arange(nB) * BS, 1, BS).astype(jnp.float32)
    k_mean = kb.astype(jnp.float32).sum(axis=3) / n_in_block[:, None]
    gate = jnp.einsum('bhtd,bhnd->bhtn', q.astype(jnp.float32), k_mean)
    q_blk = jnp.arange(T) // BS
    past = jnp.arange(nB)[None, :] < q_blk[:, None]
    gate = jnp.where(past[None, None], gate, -jnp.inf)
    ksel = max(1, min(MOBA_TOPK, nB - 1))
    _, idx = lax.top_k(gate, ksel)
    valid = idx < q_blk[None, None, :, None]
    scale = d ** -0.5
    bi = jnp.arange(B)[:, None, None, None]
    hi = jnp.arange(H)[None, :, None, None]

    def chunk(c):
        t0 = c * MOBA_Q_CHUNK
        qc = lax.dynamic_slice_in_dim(q, t0, MOBA_Q_CHUNK, axis=2)
        idc = lax.dynamic_slice_in_dim(idx, t0, MOBA_Q_CHUNK, axis=2)
        vac = lax.dynamic_slice_in_dim(valid, t0, MOBA_Q_CHUNK, axis=2)
        k_sel = kb[bi, hi, idc]
        s_sel = jnp.einsum('bhqd,bhqkld->bhqkl', qc, k_sel).astype(jnp.float32) * scale
        s_sel = jnp.where(vac[..., None], s_sel, -jnp.inf).reshape(B, H, MOBA_Q_CHUNK, ksel * BS)
        ob = t0 // BS
        k_own = lax.dynamic_index_in_dim(kb, ob, axis=2, keepdims=False)
        v_own = lax.dynamic_index_in_dim(vb, ob, axis=2, keepdims=False)
        s_own = jnp.einsum('bhqd,bhld->bhql', qc, k_own).astype(jnp.float32) * scale
        k_pos = ob * BS + jnp.arange(BS)
        q_pos = t0 + jnp.arange(MOBA_Q_CHUNK)
        s_own = jnp.where((k_pos[None, :] <= q_pos[:, None])[None, None], s_own, -jnp.inf)
        p = jax.nn.softmax(jnp.concatenate([s_sel, s_own], axis=-1), axis=-1).astype(q.dtype)
        p_sel = p[..., :ksel * BS].reshape(B, H, MOBA_Q_CHUNK, ksel, BS)
        p_own = p[..., ksel * BS:]
        v_sel = vb[bi, hi, idc]
        return (jnp.einsum('bhqkl,bhqkld->bhqd', p_sel, v_sel)
                + jnp.einsum('bhql,bhld->bhqd', p_own, v_own))

    out = lax.map(chunk, jnp.arange(T // MOBA_Q_CHUNK))
    return out.transpose(1, 0, 3, 2, 4).reshape(B, T, H * d)


def token_mixer(h, w_in, conv_w, conv_b, conv_ln_g, conv_ln_b, q_norm_g, k_norm_g,
                w_ret_o, w_conv_o, w_moba_o, w_out, cos_r, sin_r, cos_m, sin_m):
    B, T, _ = h.shape
    proj = h @ w_in
    sizes = (RET_W,) * 4 + (CONV_CH,) * 2 + (MOBA_W,) * 3 + (D_MODEL,) * N_BRANCH
    points = np.cumsum(sizes)[:-1].tolist()
    rq, rk, rv, rg, ca, cb, mq, mk, mv, gr, gc, gm = jnp.split(proj, points, axis=-1)

    q = apply_rope(rq.reshape(B, T, RET_HEADS, RET_HEAD_DIM), cos_r, sin_r)
    k = apply_rope(rk.reshape(B, T, RET_HEADS, RET_HEAD_DIM), cos_r, sin_r) * (RET_HEAD_DIM ** -0.5)
    v = rv.reshape(B, T, RET_HEADS, RET_HEAD_DIM)
    y = head_group_norm(retention_chunkwise(q, k, v)).reshape(B, T, RET_W).astype(h.dtype)
    y_ret = (jax.nn.silu(rg) * y) @ w_ret_o

    y_conv = conformer_conv(ca, cb, conv_w, conv_b, conv_ln_g, conv_ln_b) @ w_conv_o

    q = apply_rope(rms_norm(mq.reshape(B, T, MOBA_HEADS, MOBA_HEAD_DIM), q_norm_g), cos_m, sin_m)
    k = apply_rope(rms_norm(mk.reshape(B, T, MOBA_HEADS, MOBA_HEAD_DIM), k_norm_g), cos_m, sin_m)
    v = mv.reshape(B, T, MOBA_HEADS, MOBA_HEAD_DIM)
    y_moba = moba_attention(q, k, v) @ w_moba_o

    merged = (jax.nn.sigmoid(gr) * y_ret + jax.nn.sigmoid(gc) * y_conv
              + jax.nn.sigmoid(gm) * y_moba)
    return merged @ w_out


def swiglu(h, w_gate, w_up, w_down):
    return (jax.nn.silu(h @ w_gate) * (h @ w_up)) @ w_down


def moe_swiglu(h, w_router, w_e_gate, w_e_up, w_e_down):
    B, T, D = h.shape
    N = B * T
    NA = N * TOP_K
    xf = h.reshape(N, D)
    logits = (xf @ w_router).astype(jnp.float32)
    top_vals, top_idx = lax.top_k(logits, TOP_K)
    gates = jax.nn.softmax(top_vals, axis=-1)
    e_flat = top_idx.reshape(-1)
    tok = jnp.arange(NA) // TOP_K
    order = jnp.argsort(e_flat)
    se, stok, sg = e_flat[order], tok[order], gates.reshape(-1)[order]
    counts = jnp.bincount(e_flat, length=N_EXPERTS)
    padded = ((counts + MOE_GROUP - 1) // MOE_GROUP) * MOE_GROUP
    pad_end = jnp.cumsum(padded)
    pad_start = pad_end - padded
    grp_start = jnp.cumsum(counts) - counts
    dest = pad_start[se] + (jnp.arange(NA) - grp_start[se])
    P = (-(-NA // MOE_GROUP) + N_EXPERTS) * MOE_GROUP
    n_grp = P // MOE_GROUP
    xbuf = jnp.zeros((P, D), xf.dtype).at[dest].set(xf[stok])
    grp_expert = jnp.minimum(
        jnp.searchsorted(pad_end, jnp.arange(n_grp) * MOE_GROUP, side='right'), N_EXPERTS - 1)

    def expert_group(args):
        xg, e = args
        return (jax.nn.silu(xg @ w_e_gate[e]) * (xg @ w_e_up[e])) @ w_e_down[e]

    ybuf = lax.map(expert_group, (xbuf.reshape(n_grp, MOE_GROUP, D), grp_expert)).reshape(P, D)
    y = ybuf[dest] * sg[:, None].astype(ybuf.dtype)
    out = jnp.zeros((N, D), ybuf.dtype).at[stok].add(y)
    return out.reshape(B, T, D)


def setup_inputs(seed: int = 0) -> dict:
    key = jax.random.key(seed)
    ks = jax.random.split(key, 21)
    f32 = jnp.float32
    n_dense = (DEPTH + 1) // 2
    n_moe = DEPTH // 2
    res_scale = (2 * DEPTH) ** -0.5

    def nrm(k, shape, scale):
        return jax.random.normal(k, shape, f32) * scale

    return {
        'x': nrm(ks[0], (BATCH, SEQ, D_MODEL), 1.0),
        'g_mix': 1.0 + nrm(ks[1], (DEPTH, D_MODEL), 0.01),
        'w_in': nrm(ks[2], (DEPTH, D_MODEL, IN_COLS), D_MODEL ** -0.5),
        'conv_w': nrm(ks[3], (DEPTH, CONV_WIDTH, CONV_CH), CONV_WIDTH ** -0.5),
        'conv_b': nrm(ks[4], (DEPTH, CONV_CH), 0.01),
        'conv_ln_g': 1.0 + nrm(ks[5], (DEPTH, CONV_CH), 0.01),
        'conv_ln_b': nrm(ks[6], (DEPTH, CONV_CH), 0.01),
        'q_norm_g': 1.0 + nrm(ks[7], (DEPTH, MOBA_HEAD_DIM), 0.01),
        'k_norm_g': 1.0 + nrm(ks[8], (DEPTH, MOBA_HEAD_DIM), 0.01),
        'w_ret_o': nrm(ks[9], (DEPTH, RET_W, D_MODEL), RET_W ** -0.5),
        'w_conv_o': nrm(ks[10], (DEPTH, CONV_CH, D_MODEL), CONV_CH ** -0.5),
        'w_moba_o': nrm(ks[11], (DEPTH, MOBA_W, D_MODEL), MOBA_W ** -0.5),
        'w_out': nrm(ks[12], (DEPTH, D_MODEL, D_MODEL), D_MODEL ** -0.5 * res_scale),
        'g_ffn': 1.0 + nrm(ks[13], (DEPTH, D_MODEL), 0.01),
        'w_ff_gate': nrm(ks[14], (n_dense, D_MODEL, FF_DENSE), D_MODEL ** -0.5),
        'w_ff_up': nrm(ks[15], (n_dense, D_MODEL, FF_DENSE), D_MODEL ** -0.5),
        'w_ff_down': nrm(ks[16], (n_dense, FF_DENSE, D_MODEL), FF_DENSE ** -0.5 * res_scale),
        'w_router': nrm(ks[17], (n_moe, D_MODEL, N_EXPERTS), D_MODEL ** -0.5),
        'w_e_gate': nrm(ks[18], (n_moe, N_EXPERTS, D_MODEL, FF_EXPERT), D_MODEL ** -0.5),
        'w_e_up': nrm(ks[19], (n_moe, N_EXPERTS, D_MODEL, FF_EXPERT), D_MODEL ** -0.5),
        'w_e_down': nrm(ks[20], (n_moe, N_EXPERTS, FF_EXPERT, D_MODEL), FF_EXPERT ** -0.5 * res_scale),
    }


def reference(x, g_mix, w_in, conv_w, conv_b, conv_ln_g, conv_ln_b, q_norm_g, k_norm_g,
              w_ret_o, w_conv_o, w_moba_o, w_out, g_ffn, w_ff_gate, w_ff_up, w_ff_down,
              w_router, w_e_gate, w_e_up, w_e_down):
    T = x.shape[1]
    cos_r, sin_r = rope_tables(T, RET_HEAD_DIM)
    cos_m, sin_m = rope_tables(T, MOBA_HEAD_DIM)
    for l in range(DEPTH):
        h = rms_norm(x, g_mix[l])
        x = x + token_mixer(h, w_in[l], conv_w[l], conv_b[l], conv_ln_g[l], conv_ln_b[l],
                            q_norm_g[l], k_norm_g[l], w_ret_o[l], w_conv_o[l], w_moba_o[l],
                            w_out[l], cos_r, sin_r, cos_m, sin_m)
        h = rms_norm(x, g_ffn[l])
        j = l // 2
        if l % 2 == 0:
            x = x + swiglu(h, w_ff_gate[j], w_ff_up[j], w_ff_down[j])
        else:
            x = x + moe_swiglu(h, w_router[j], w_e_gate[j], w_e_up[j], w_e_down[j])
    return x
```

```python
import functools

import jax
import jax.numpy as jnp
from jax import lax
from jax.experimental import pallas as pl
from jax.experimental.pallas import tpu as pltpu

F32 = jnp.float32
BF16 = jnp.bfloat16

D_MODEL = 1024
RET_HEADS = 4
RET_HEAD_DIM = 128
RET_W = RET_HEADS * RET_HEAD_DIM
RET_CHUNK = 128
CONV_CH = 512
CONV_WIDTH = 31
MOBA_HEADS = 8
MOBA_HEAD_DIM = 64
MOBA_W = MOBA_HEADS * MOBA_HEAD_DIM
MOBA_BLOCK = 256
MOBA_TOPK = 3
ROPE_THETA = 10000.0
N_EXPERTS = 8
TOP_K = 2
NORM_EPS = 1e-6
GROUP_NORM_EPS = 1e-5

OFF_RQ, OFF_RK, OFF_RV, OFF_RG = 0, 512, 1024, 1536
OFF_CA, OFF_CB = 2048, 2560
OFF_MQ, OFF_MK, OFF_MV = 3072, 3584, 4096
OFF_GATES = 4608
IN_COLS = OFF_GATES + 3 * D_MODEL

LANES = 128
CONV_HALO = 32
VMEM_LIMIT_BYTES = 56 * 1024 * 1024
NEG_BIG = -1e30

_NT = (((1,), (1,)), ((), ()))
_TN = (((0,), (0,)), ((), ()))


def _cparams(*sem):
    return pltpu.CompilerParams(dimension_semantics=sem, vmem_limit_bytes=VMEM_LIMIT_BYTES)


def _sigmoid(x):
    return 1.0 / (1.0 + jnp.exp(-x))


def _inproj_kernel(x_ref, g_ref, w_ref, o_ref, h_ref):
    @pl.when(pl.program_id(1) == 0)
    def _():
        x = x_ref[...]
        ms = jnp.mean(x * x, axis=-1, keepdims=True)
        h_ref[...] = (x * lax.rsqrt(ms + NORM_EPS) * g_ref[...]).astype(BF16)

    o_ref[...] = jnp.dot(h_ref[...], w_ref[...], preferred_element_type=F32).astype(o_ref.dtype)


def _inproj(xf, g, w):
    n, d = xf.shape
    c = w.shape[1]
    tm, tn = 512, 1536
    return pl.pallas_call(
        _inproj_kernel,
        grid=(n // tm, c // tn),
        in_specs=[pl.BlockSpec((tm, d), lambda i, j: (i, 0)),
                  pl.BlockSpec((1, d), lambda i, j: (0, 0)),
                  pl.BlockSpec((d, tn), lambda i, j: (0, j))],
        out_specs=pl.BlockSpec((tm, tn), lambda i, j: (i, j)),
        out_shape=jax.ShapeDtypeStruct((n, c), BF16),
        scratch_shapes=[pltpu.VMEM((tm, d), BF16)],
        compiler_params=_cparams("parallel", "arbitrary"),
        name="inproj",
    )(xf, g.reshape(1, d), w)


def _ret_kernel(q_ref, k_ref, v_ref, g_ref, cos_ref, sin_ref, dec_ref, wst_ref, wcr_ref, gch_ref,
                o_ref, *, n_chunks):
    c_len = RET_CHUNK

    def body(c, state):
        r = pl.ds(pl.multiple_of(c * c_len, c_len), c_len)
        cs = cos_ref[r, :]
        sn = sin_ref[r, :]
        q = q_ref[r, :].astype(F32)
        k = k_ref[r, :].astype(F32)
        q = q * cs + pltpu.roll(q, RET_HEAD_DIM // 2, 1) * sn
        k = (k * cs + pltpu.roll(k, RET_HEAD_DIM // 2, 1) * sn) * (RET_HEAD_DIM ** -0.5)
        qb = q.astype(BF16)
        kb = k.astype(BF16)
        vb = v_ref[r, :]
        s = lax.dot_general(qb, kb, _NT, preferred_element_type=F32) * dec_ref[0]
        y = jnp.dot(s.astype(BF16), vb, preferred_element_type=F32)
        y = y + jnp.dot(qb, state.astype(BF16), preferred_element_type=F32) * wcr_ref[0]
        kw = (k * wst_ref[0]).astype(BF16)
        kv = lax.dot_general(kw, vb, _TN, preferred_element_type=F32)
        state = gch_ref[0] * state + kv
        mu = jnp.mean(y, axis=-1, keepdims=True)
        yc = y - mu
        var = jnp.mean(yc * yc, axis=-1, keepdims=True)
        yn = yc * lax.rsqrt(var + GROUP_NORM_EPS)
        g = g_ref[r, :].astype(F32)
        o_ref[r, :] = (g * _sigmoid(g) * yn).astype(o_ref.dtype)
        return state

    lax.fori_loop(0, n_chunks, body, jnp.zeros((RET_HEAD_DIM, RET_HEAD_DIM), F32))


def _retention(proj, tabs, batch, seq):
    n = proj.shape[0]
    hd = RET_HEAD_DIM

    def col(off):
        return pl.BlockSpec((seq, hd), lambda b, h, off=off: (b, off // hd + h))

    tab = pl.BlockSpec((seq, hd), lambda b, h: (0, 0))
    htab = pl.BlockSpec((1, hd, hd), lambda b, h: (h, 0, 0))
    return pl.pallas_call(
        functools.partial(_ret_kernel, n_chunks=seq // RET_CHUNK),
        grid=(batch, RET_HEADS),
        in_specs=[col(OFF_RQ), col(OFF_RK), col(OFF_RV), col(OFF_RG), tab, tab, htab, htab, htab, htab],
        out_specs=pl.BlockSpec((seq, hd), lambda b, h: (b, h)),
        out_shape=jax.ShapeDtypeStruct((n, RET_W), BF16),
        compiler_params=_cparams("parallel", "parallel"),
        name="retention",
    )(proj, proj, proj, proj, tabs["ret_cos"], tabs["ret_sin"], tabs["ret_decay"], tabs["ret_wstate"],
      tabs["ret_wcross"], tabs["ret_gchunk"])


def _conv_kernel(ca_ref, cb_ref, w_ref, b_ref, lg_ref, lb_ref, o_ref, ext_ref, *, t_tile, r_chunk):
    t = pl.program_id(1)

    @pl.when(t == 0)
    def _():
        ext_ref[0:CONV_HALO, :] = jnp.zeros((CONV_HALO, CONV_CH), F32)

    @pl.when(t > 0)
    def _():
        ext_ref[0:CONV_HALO, :] = ext_ref[t_tile:t_tile + CONV_HALO, :]

    ca = ca_ref[...].astype(F32)
    cb = cb_ref[...].astype(F32)
    ext_ref[CONV_HALO:, :] = ca * _sigmoid(cb)

    first_tap = CONV_HALO - (CONV_WIDTH - 1)
    bias = b_ref[...]
    lg = lg_ref[...]
    lb = lb_ref[...]
    for r0 in range(0, t_tile, r_chunk):
        acc = jnp.zeros((r_chunk, CONV_CH), F32) + bias
        for j in range(CONV_WIDTH):
            acc = acc + ext_ref[pl.ds(r0 + first_tap + j, r_chunk), :] * w_ref[j:j + 1, :]
        mu = jnp.mean(acc, axis=-1, keepdims=True)
        yc = acc - mu
        var = jnp.mean(yc * yc, axis=-1, keepdims=True)
        y = yc * lax.rsqrt(var + NORM_EPS) * lg + lb
        o_ref[r0:r0 + r_chunk, :] = (y * _sigmoid(y)).astype(o_ref.dtype)


def _conv_branch(proj, conv_w, conv_b, ln_g, ln_b, batch, seq):
    n = proj.shape[0]
    t_tile = 512
    tiles = seq // t_tile
    w = jnp.zeros((CONV_HALO, CONV_CH), F32).at[:CONV_WIDTH].set(conv_w)
    vec = pl.BlockSpec((1, CONV_CH), lambda b, t: (0, 0))

    def col(off):
        return pl.BlockSpec((t_tile, CONV_CH), lambda b, t, off=off: (b * tiles + t, off // CONV_CH))

    return pl.pallas_call(
        functools.partial(_conv_kernel, t_tile=t_tile, r_chunk=64),
        grid=(batch, tiles),
        in_specs=[col(OFF_CA), col(OFF_CB), pl.BlockSpec((CONV_HALO, CONV_CH), lambda b, t: (0, 0)),
                  vec, vec, vec],
        out_specs=pl.BlockSpec((t_tile, CONV_CH), lambda b, t: (b * tiles + t, 0)),
        out_shape=jax.ShapeDtypeStruct((n, CONV_CH), BF16),
        scratch_shapes=[pltpu.VMEM((CONV_HALO + t_tile, CONV_CH), F32)],
        compiler_params=_cparams("parallel", "arbitrary"),
        name="conv_branch",
    )(proj, proj, w, conv_b.reshape(1, -1), ln_g.reshape(1, -1), ln_b.reshape(1, -1))


def _moba_kernel(q_ref, k_ref, v_ref, qg_ref, kg_ref, cos_ref, sin_ref, o_ref,
                 qf_s, q0_s, q1_s, k_s, km_s, *, n_blocks):
    bs = MOBA_BLOCK
    hd = MOBA_HEAD_DIM
    lane = lax.broadcasted_iota(jnp.int32, (1, LANES), 1)
    head0 = lane < hd
    first_half = (lane % hd) < (hd // 2)
    qg = qg_ref[...]
    kg = kg_ref[...]

    def norm_rope(x, g, cs, sn):
        x2 = x * x
        s0 = jnp.sum(jnp.where(head0, x2, 0.0), axis=-1, keepdims=True)
        s1 = jnp.sum(jnp.where(head0, 0.0, x2), axis=-1, keepdims=True)
        ms = jnp.where(head0, s0, s1) * (1.0 / hd)
        y = x * lax.rsqrt(ms + NORM_EPS) * g
        partner = jnp.where(first_half, pltpu.roll(y, LANES - hd // 2, 1), pltpu.roll(y, hd // 2, 1))
        return y * cs + partner * sn

    km_s[...] = jnp.zeros(km_s.shape, F32)

    def prologue(j, carry):
        r = pl.ds(pl.multiple_of(j * bs, bs), bs)
        cs = cos_ref[r, :]
        sn = sin_ref[r, :]
        qn = norm_rope(q_ref[r, :].astype(F32), qg, cs, sn)
        kn = norm_rope(k_ref[r, :].astype(F32), kg, cs, sn)
        qf_s[r, :] = qn
        qsc = qn * (hd ** -0.5)
        q0_s[r, :] = jnp.where(head0, qsc, 0.0).astype(BF16)
        q1_s[r, :] = jnp.where(head0, 0.0, qsc).astype(BF16)
        k_s[r, :] = kn.astype(BF16)
        kmean = jnp.sum(kn, axis=0, keepdims=True) * (1.0 / bs)
        km_s[pl.ds(j, 1), :] = jnp.where(head0, kmean, 0.0)
        km_s[pl.ds(n_blocks + j, 1), :] = jnp.where(head0, 0.0, kmean)
        return carry

    lax.fori_loop(0, n_blocks, prologue, 0)

    row_i = lax.broadcasted_iota(jnp.int32, (bs, bs), 0)
    col_i = lax.broadcasted_iota(jnp.int32, (bs, bs), 1)
    causal = row_i >= col_i

    def q_block(b, carry):
        rq = pl.ds(pl.multiple_of(b * bs, bs), bs)
        gate = lax.dot_general(qf_s[rq, :], km_s[...], _NT, precision=lax.Precision.HIGHEST,
                               preferred_element_type=F32)
        outs = []
        for h in range(2):
            base = h * n_blocks
            lane_j = lane - base

            def rank_body(jp, rank):
                col = jnp.sum(jnp.where(lane == base + jp, gate, 0.0), axis=-1, keepdims=True)
                ahead = jnp.where(col > gate, 1.0, 0.0) + jnp.where((col == gate) & (jp < lane_j), 1.0, 0.0)
                return rank + ahead

            rank = lax.fori_loop(0, b, rank_body, jnp.zeros((bs, LANES), F32))
            sel_f = jnp.where(rank < float(MOBA_TOPK), 1.0, 0.0)

            qh = (q0_s if h == 0 else q1_s)[rq, :]
            s = lax.dot_general(qh, k_s[rq, :], _NT, preferred_element_type=F32)
            s = jnp.where(causal, s, NEG_BIG)
            m = jnp.max(s, axis=-1, keepdims=True)
            p = jnp.exp(s - m)
            l = jnp.sum(p, axis=-1, keepdims=True)
            acc = jnp.dot(p.astype(BF16), v_ref[rq, :], preferred_element_type=F32)

            def kv_body(j, mla):
                m, l, acc = mla
                rk = pl.ds(pl.multiple_of(j * bs, bs), bs)
                s = lax.dot_general(qh, k_s[rk, :], _NT, preferred_element_type=F32)
                sel = jnp.sum(jnp.where(lane == base + j, sel_f, 0.0), axis=-1, keepdims=True) > 0.5
                s = jnp.where(sel, s, NEG_BIG)
                m_new = jnp.maximum(m, jnp.max(s, axis=-1, keepdims=True))
                alpha = jnp.exp(m - m_new)
                p = jnp.exp(s - m_new)
                l = alpha * l + jnp.sum(p, axis=-1, keepdims=True)
                acc = alpha * acc + jnp.dot(p.astype(BF16), v_ref[rk, :], preferred_element_type=F32)
                return m_new, l, acc

            m, l, acc = lax.fori_loop(0, b, kv_body, (m, l, acc))
            outs.append(acc / l)
        o_ref[rq, :] = jnp.where(head0, outs[0], outs[1]).astype(o_ref.dtype)
        return carry

    lax.fori_loop(0, n_blocks, q_block, 0)


def _moba(proj, q_gain, k_gain, tabs, batch, seq):
    n = proj.shape[0]
    assert seq % MOBA_BLOCK == 0
    n_blocks = seq // MOBA_BLOCK
    assert 2 * n_blocks <= LANES
    pairs = MOBA_W // LANES

    def col(off):
        return pl.BlockSpec((seq, LANES), lambda b, p, off=off: (b, off // LANES + p))

    tab = pl.BlockSpec((seq, LANES), lambda b, p: (0, 0))
    vec = pl.BlockSpec((1, LANES), lambda b, p: (0, 0))
    return pl.pallas_call(
        functools.partial(_moba_kernel, n_blocks=n_blocks),
        grid=(batch, pairs),
        in_specs=[col(OFF_MQ), col(OFF_MK), col(OFF_MV), vec, vec, tab, tab],
        out_specs=pl.BlockSpec((seq, LANES), lambda b, p: (b, p)),
        out_shape=jax.ShapeDtypeStruct((n, MOBA_W), BF16),
        scratch_shapes=[pltpu.VMEM((seq, LANES), F32), pltpu.VMEM((seq, LANES), BF16),
                        pltpu.VMEM((seq, LANES), BF16), pltpu.VMEM((seq, LANES), BF16),
                        pltpu.VMEM((LANES, LANES), F32)],
        compiler_params=_cparams("parallel", "parallel"),
        name="moba",
    )(proj, proj, proj, jnp.tile(q_gain, 2).reshape(1, LANES), jnp.tile(k_gain, 2).reshape(1, LANES),
      tabs["moba_cos"], tabs["moba_sin"])


def _merge_kernel(x_ref, ga_ref, gb_ref, yr_ref, yc_ref, ym_ref, wr_ref, wc_ref, wm_ref, wo_ref, gf_ref,
                  *rest, with_router):
    if with_router:
        wrt_ref, xo_ref, h_ref, lg_ref = rest
    else:
        xo_ref, h_ref = rest
    d = D_MODEL
    ga = ga_ref[...].astype(F32)
    gb = gb_ref[...].astype(F32)
    g_ret = ga[:, :d]
    g_conv = jnp.concatenate([ga[:, d:], gb[:, :d // 2]], axis=-1)
    g_moba = gb[:, d // 2:]
    merged = (_sigmoid(g_ret) * jnp.dot(yr_ref[...], wr_ref[...], preferred_element_type=F32)
              + _sigmoid(g_conv) * jnp.dot(yc_ref[...], wc_ref[...], preferred_element_type=F32)
              + _sigmoid(g_moba) * jnp.dot(ym_ref[...], wm_ref[...], preferred_element_type=F32))
    x = x_ref[...] + jnp.dot(merged.astype(BF16), wo_ref[...], preferred_element_type=F32)
    xo_ref[...] = x
    ms = jnp.mean(x * x, axis=-1, keepdims=True)
    h = x * lax.rsqrt(ms + NORM_EPS) * gf_ref[...]
    h_ref[...] = h.astype(BF16)
    if with_router:
        lg_ref[...] = jnp.dot(h, wrt_ref[...], precision=lax.Precision.HIGHEST, preferred_element_type=F32)


def _merge(xf, proj, yr, yc, ym, w_ret_o, w_conv_o, w_moba_o, w_out, g_ffn, w_router):
    n, d = xf.shape
    tm = 512
    gw = 3 * d // 2
    with_router = w_router is not None
    row = lambda w: pl.BlockSpec((tm, w), lambda i: (i, 0))
    full = lambda a: pl.BlockSpec(a.shape, lambda i: (0,) * a.ndim)
    wr, wc, wm, wo = (w.astype(BF16) for w in (w_ret_o, w_conv_o, w_moba_o, w_out))
    gf = g_ffn.reshape(1, d)
    args = [xf, proj, proj, yr, yc, ym, wr, wc, wm, wo, gf]
    in_specs = [row(d),
                pl.BlockSpec((tm, gw), lambda i: (i, OFF_GATES // gw)),
                pl.BlockSpec((tm, gw), lambda i: (i, OFF_GATES // gw + 1)),
                row(RET_W), row(CONV_CH), row(MOBA_W), full(wr), full(wc), full(wm), full(wo), full(gf)]
    out_shape = [jax.ShapeDtypeStruct((n, d), F32), jax.ShapeDtypeStruct((n, d), BF16)]
    out_specs = [row(d), row(d)]
    if with_router:
        wrt = jnp.zeros((d, LANES), F32).at[:, :N_EXPERTS].set(w_router)
        args.append(wrt)
        in_specs.append(full(wrt))
        out_shape.append(jax.ShapeDtypeStruct((n, LANES), F32))
        out_specs.append(row(LANES))
    return pl.pallas_call(
        functools.partial(_merge_kernel, with_router=with_router),
        grid=(n // tm,),
        in_specs=in_specs, out_specs=out_specs, out_shape=out_shape,
        compiler_params=_cparams("parallel"),
        name="merge",
    )(*args)


def _ffn_kernel(x_ref, h_ref, wg_ref, wu_ref, wd_ref, o_ref):
    f = pl.program_id(1)
    h = h_ref[...]
    g = jnp.dot(h, wg_ref[...], preferred_element_type=F32)
    u = jnp.dot(h, wu_ref[...], preferred_element_type=F32)
    a = (g * _sigmoid(g) * u).astype(BF16)
    dlt = jnp.dot(a, wd_ref[...], preferred_element_type=F32)

    @pl.when(f == 0)
    def _():
        o_ref[...] = x_ref[...] + dlt

    @pl.when(f > 0)
    def _():
        o_ref[...] += dlt


def _dense_ffn(xf, h, w_gate, w_up, w_down):
    n, d = xf.shape
    ff = w_gate.shape[1]
    tm, tf = 512, 1408
    assert ff % tf == 0
    return pl.pallas_call(
        _ffn_kernel,
        grid=(n // tm, ff // tf),
        in_specs=[pl.BlockSpec((tm, d), lambda i, f: (i, 0)),
                  pl.BlockSpec((tm, d), lambda i, f: (i, 0)),
                  pl.BlockSpec((d, tf), lambda i, f: (0, f)),
                  pl.BlockSpec((d, tf), lambda i, f: (0, f)),
                  pl.BlockSpec((tf, d), lambda i, f: (f, 0))],
        out_specs=pl.BlockSpec((tm, d), lambda i, f: (i, 0)),
        out_shape=jax.ShapeDtypeStruct((n, d), F32),
        compiler_params=_cparams("parallel", "arbitrary"),
        name="dense_ffn",
    )(xf, h, w_gate.astype(BF16), w_up.astype(BF16), w_down.astype(BF16))


def _moe_kernel(te_ref, act_ref, x_ref, wg_ref, wu_ref, wd_ref, o_ref):
    i = pl.program_id(0)
    f = pl.program_id(1)
    active = act_ref[i] > 0

    @pl.when(active)
    def _():
        x = x_ref[...]
        g = jnp.dot(x, wg_ref[0], preferred_element_type=F32)
        u = jnp.dot(x, wu_ref[0], preferred_element_type=F32)
        a = (g * _sigmoid(g) * u).astype(BF16)
        dlt = jnp.dot(a, wd_ref[0], preferred_element_type=F32)

        @pl.when(f == 0)
        def _():
            o_ref[...] = dlt

        @pl.when(f > 0)
        def _():
            o_ref[...] += dlt

    @pl.when(jnp.logical_not(active) & (f == 0))
    def _():
        o_ref[...] = jnp.zeros(o_ref.shape, o_ref.dtype)


def _moe_experts(xbuf, tile_expert, tile_active, w_gate, w_up, w_down, tm):
    p, d = xbuf.shape
    ff = w_gate.shape[2]
    tf = 896
    assert ff % tf == 0
    nf = ff // tf

    def fsel(f, act, i):
        return jnp.where(act[i] > 0, f, nf - 1)

    grid_spec = pltpu.PrefetchScalarGridSpec(
        num_scalar_prefetch=2,
        grid=(p // tm, nf),
        in_specs=[pl.BlockSpec((tm, d), lambda i, f, te, act: (i, 0)),
                  pl.BlockSpec((1, d, tf), lambda i, f, te, act: (te[i], 0, fsel(f, act, i))),
                  pl.BlockSpec((1, d, tf), lambda i, f, te, act: (te[i], 0, fsel(f, act, i))),
                  pl.BlockSpec((1, tf, d), lambda i, f, te, act: (te[i], fsel(f, act, i), 0))],
        out_specs=pl.BlockSpec((tm, d), lambda i, f, te, act: (i, 0)),
    )
    return pl.pallas_call(
        _moe_kernel,
        grid_spec=grid_spec,
        out_shape=jax.ShapeDtypeStruct((p, d), F32),
        compiler_params=_cparams("arbitrary", "arbitrary"),
        name="moe_experts",
    )(tile_expert, tile_active, xbuf, w_gate.astype(BF16), w_up.astype(BF16), w_down.astype(BF16))


def _moe(xf, h, logits, w_gate, w_up, w_down):
    n, d = xf.shape
    na = n * TOP_K
    tm = 512
    n_tiles = na // tm + N_EXPERTS
    top_vals, top_idx = lax.top_k(logits, TOP_K)
    gates = jax.nn.softmax(top_vals, axis=-1)
    e_flat = top_idx.reshape(-1)
    onehot = (e_flat[:, None] == jnp.arange(N_EXPERTS, dtype=e_flat.dtype)[None, :]).astype(jnp.int32)
    csum = jnp.cumsum(onehot, axis=0)
    pos_in_e = jnp.take_along_axis(csum, e_flat[:, None], axis=1)[:, 0] - 1
    counts = csum[-1]
    tiles_per_e = (counts + tm - 1) // tm
    tile_end = jnp.cumsum(tiles_per_e)
    tile_start = tile_end - tiles_per_e
    dest = tile_start[e_flat] * tm + pos_in_e
    tile_ids = jnp.arange(n_tiles, dtype=jnp.int32)
    tile_expert = jnp.minimum(jnp.searchsorted(tile_end, tile_ids, side="right"), N_EXPERTS - 1).astype(jnp.int32)
    tile_active = (tile_ids < tile_end[-1]).astype(jnp.int32)
    p = n_tiles * tm
    src = jnp.zeros((p,), jnp.int32).at[dest].set(jnp.arange(na, dtype=jnp.int32) // TOP_K)
    valid = jnp.zeros((p,), jnp.bool_).at[dest].set(True)
    xbuf = jnp.where(valid[:, None], h[src], jnp.zeros((), h.dtype))
    ybuf = _moe_experts(xbuf, tile_expert, tile_active, w_gate, w_up, w_down, tm)
    y = ybuf[dest].reshape(n, TOP_K, d) * gates[:, :, None]
    return xf + y.sum(axis=1)


def _rope(seq, dim):
    inv = 1.0 / (ROPE_THETA ** (jnp.arange(0, dim, 2, dtype=F32) / dim))
    ang = jnp.arange(seq, dtype=F32)[:, None] * inv[None, :]
    return jnp.cos(ang), jnp.sin(ang)


def _tables(seq):
    cos_r, sin_r = _rope(seq, RET_HEAD_DIM)
    cos_m, sin_m = _rope(seq, MOBA_HEAD_DIM)
    c_len = RET_CHUNK
    log_g = jnp.log(1.0 - jnp.exp2(-5.0 - jnp.arange(RET_HEADS, dtype=F32)))
    pos = jnp.arange(c_len, dtype=F32)
    diff = pos[:, None] - pos[None, :]
    decay = jnp.where(diff[None] >= 0, jnp.exp(diff[None] * log_g[:, None, None]), 0.0)
    w_state = jnp.exp((c_len - 1.0 - pos)[None, :] * log_g[:, None])
    w_cross = jnp.exp((pos + 1.0)[None, :] * log_g[:, None])
    g_chunk = jnp.exp(c_len * log_g)
    bcast = lambda a: jnp.broadcast_to(a[:, :, None], (RET_HEADS, c_len, RET_HEAD_DIM))
    return {
        "ret_cos": jnp.concatenate([cos_r, cos_r], axis=-1),
        "ret_sin": jnp.concatenate([-sin_r, sin_r], axis=-1),
        "ret_decay": decay,
        "ret_wstate": bcast(w_state),
        "ret_wcross": bcast(w_cross),
        "ret_gchunk": jnp.broadcast_to(g_chunk[:, None, None], (RET_HEADS, RET_HEAD_DIM, RET_HEAD_DIM)),
        "moba_cos": jnp.tile(cos_m, (1, 2 * LANES // MOBA_HEAD_DIM)),
        "moba_sin": jnp.tile(jnp.concatenate([-sin_m, sin_m], axis=-1), (1, LANES // MOBA_HEAD_DIM)),
    }


def kernel(x, g_mix, w_in, conv_w, conv_b, conv_ln_g, conv_ln_b, q_norm_g, k_norm_g, w_ret_o, w_conv_o,
           w_moba_o, w_out, g_ffn, w_ff_gate, w_ff_up, w_ff_down, w_router, w_e_gate, w_e_up, w_e_down):
    batch, seq, d = x.shape
    depth = g_mix.shape[0]
    assert d == D_MODEL and w_in.shape[2] == IN_COLS
    xf = x.reshape(batch * seq, d)
    tabs = _tables(seq)
    for l in range(depth):
        proj = _inproj(xf, g_mix[l], w_in[l].astype(BF16))
        yr = _retention(proj, tabs, batch, seq)
        yc = _conv_branch(proj, conv_w[l], conv_b[l], conv_ln_g[l], conv_ln_b[l], batch, seq)
        ym = _moba(proj, q_norm_g[l], k_norm_g[l], tabs, batch, seq)
        j = l // 2
        if l % 2 == 0:
            xf, h = _merge(xf, proj, yr, yc, ym, w_ret_o[l], w_conv_o[l], w_moba_o[l], w_out[l], g_ffn[l], None)
            xf = _dense_ffn(xf, h, w_ff_gate[j], w_ff_up[j], w_ff_down[j])
        else:
            xf, h, logits = _merge(xf, proj, yr, yc, ym, w_ret_o[l], w_conv_o[l], w_moba_o[l], w_out[l],
                                   g_ffn[l], w_router[j])
            xf = _moe(xf, h, logits[:, :N_EXPERTS], w_e_gate[j], w_e_up[j], w_e_down[j])
    return xf.reshape(batch, seq, d)
```

```python
import functools

import jax
import jax.numpy as jnp
from jax import lax
from jax.experimental import pallas as pl
from jax.experimental.pallas import tpu as pltpu

F32 = jnp.float32
BF16 = jnp.bfloat16

D_MODEL = 1024
RET_HEADS = 4
RET_HEAD_DIM = 128
RET_W = RET_HEADS * RET_HEAD_DIM
RET_CHUNK = 128
CONV_CH = 512
CONV_WIDTH = 31
MOBA_HEADS = 8
MOBA_HEAD_DIM = 64
MOBA_W = MOBA_HEADS * MOBA_HEAD_DIM
MOBA_BLOCK = 256
MOBA_TOPK = 3
ROPE_THETA = 10000.0
N_EXPERTS = 8
TOP_K = 2
NORM_EPS = 1e-6
GROUP_NORM_EPS = 1e-5

OFF_RQ, OFF_RK, OFF_RV, OFF_RG = 0, 512, 1024, 1536
OFF_CA, OFF_CB = 2048, 2560
OFF_MQ, OFF_MK, OFF_MV = 3072, 3584, 4096
OFF_GATES = 4608
IN_COLS = OFF_GATES + 3 * D_MODEL

LANES = 128
CONV_HALO = 32
VMEM_LIMIT_BYTES = 56 * 1024 * 1024
NEG_BIG = -1e30
LOG2_E = 1.4426950408889634

_NT = (((1,), (1,)), ((), ()))
_TN = (((0,), (0,)), ((), ()))


def _cparams(*sem):
    return pltpu.CompilerParams(dimension_semantics=sem, vmem_limit_bytes=VMEM_LIMIT_BYTES)


def _sigmoid(x):
    return 1.0 / (1.0 + jnp.exp(-x))


def _pack_bf16_pairs(x):
    w = x.shape[1] // 2
    bits = pltpu.bitcast(x.astype(BF16).astype(F32), jnp.uint32)
    return lax.shift_right_logical(bits[:, :w], jnp.uint32(16)) | bits[:, w:]


def _unpack_bf16_pairs(words):
    left = pltpu.bitcast(lax.shift_left(words, jnp.uint32(16)), F32)
    right = pltpu.bitcast(words & jnp.uint32(0xFFFF0000), F32)
    return left.astype(BF16), right.astype(BF16)


def _inproj_kernel(x_ref, g_ref, w_ref, o_ref, h_ref):
    @pl.when(pl.program_id(1) == 0)
    def _():
        x = x_ref[...]
        ms = jnp.mean(x * x, axis=-1, keepdims=True)
        h_ref[...] = (x * lax.rsqrt(ms + NORM_EPS) * g_ref[...]).astype(BF16)

    o_ref[...] = jnp.dot(h_ref[...], w_ref[...], preferred_element_type=F32).astype(o_ref.dtype)


def _inproj(xf, g, w):
    n, d = xf.shape
    c = w.shape[1]
    tm, tn = 512, 1536
    return pl.pallas_call(
        _inproj_kernel,
        grid=(n // tm, c // tn),
        in_specs=[pl.BlockSpec((tm, d), lambda i, j: (i, 0)),
                  pl.BlockSpec((1, d), lambda i, j: (0, 0)),
                  pl.BlockSpec((d, tn), lambda i, j: (0, j))],
        out_specs=pl.BlockSpec((tm, tn), lambda i, j: (i, j)),
        out_shape=jax.ShapeDtypeStruct((n, c), BF16),
        scratch_shapes=[pltpu.VMEM((tm, d), BF16)],
        compiler_params=_cparams("parallel", "arbitrary"),
        name="inproj",
    )(xf, g.reshape(1, d), w)


def _ret_kernel(q_ref, k_ref, v_ref, g_ref, cos_ref, sin_ref, dec_ref, wst_ref, wcr_ref, gch_ref,
                o_ref, *, n_chunks):
    c_len = RET_CHUNK

    def body(c, state):
        r = pl.ds(pl.multiple_of(c * c_len, c_len), c_len)
        cs = cos_ref[r, :]
        sn = sin_ref[r, :]
        q = q_ref[r, :].astype(F32)
        k = k_ref[r, :].astype(F32)
        q = q * cs + pltpu.roll(q, RET_HEAD_DIM // 2, 1) * sn
        k = (k * cs + pltpu.roll(k, RET_HEAD_DIM // 2, 1) * sn) * (RET_HEAD_DIM ** -0.5)
        qb = q.astype(BF16)
        kb = k.astype(BF16)
        vb = v_ref[r, :]
        s = lax.dot_general(qb, kb, _NT, preferred_element_type=F32) * dec_ref[0]
        y = jnp.dot(s.astype(BF16), vb, preferred_element_type=F32)
        y = y + jnp.dot(qb, state.astype(BF16), preferred_element_type=F32) * wcr_ref[0]
        kw = (k * wst_ref[0]).astype(BF16)
        kv = lax.dot_general(kw, vb, _TN, preferred_element_type=F32)
        state = gch_ref[0] * state + kv
        mu = jnp.mean(y, axis=-1, keepdims=True)
        yc = y - mu
        var = jnp.mean(yc * yc, axis=-1, keepdims=True)
        yn = yc * lax.rsqrt(var + GROUP_NORM_EPS)
        g = g_ref[r, :].astype(F32)
        o_ref[r, :] = (g * _sigmoid(g) * yn).astype(o_ref.dtype)
        return state

    lax.fori_loop(0, n_chunks, body, jnp.zeros((RET_HEAD_DIM, RET_HEAD_DIM), F32))


def _retention(proj, tabs, batch, seq):
    n = proj.shape[0]
    hd = RET_HEAD_DIM

    def col(off):
        return pl.BlockSpec((seq, hd), lambda b, h, off=off: (b, off // hd + h))

    tab = pl.BlockSpec((seq, hd), lambda b, h: (0, 0))
    htab = pl.BlockSpec((1, hd, hd), lambda b, h: (h, 0, 0))
    return pl.pallas_call(
        functools.partial(_ret_kernel, n_chunks=seq // RET_CHUNK),
        grid=(batch, RET_HEADS),
        in_specs=[col(OFF_RQ), col(OFF_RK), col(OFF_RV), col(OFF_RG), tab, tab, htab, htab, htab, htab],
        out_specs=pl.BlockSpec((seq, hd), lambda b, h: (b, h)),
        out_shape=jax.ShapeDtypeStruct((n, RET_W), BF16),
        compiler_params=_cparams("parallel", "parallel"),
        name="retention",
    )(proj, proj, proj, proj, tabs["ret_cos"], tabs["ret_sin"], tabs["ret_decay"], tabs["ret_wstate"],
      tabs["ret_wcross"], tabs["ret_gchunk"])


def _conv_kernel(ca_ref, cb_ref, w_ref, b_ref, lg_ref, lb_ref, o_ref, ext_ref, *, t_tile, r_chunk):
    t = pl.program_id(1)

    @pl.when(t == 0)
    def _():
        ext_ref[0:CONV_HALO, :] = jnp.zeros((CONV_HALO, CONV_CH), F32)

    @pl.when(t > 0)
    def _():
        ext_ref[0:CONV_HALO, :] = ext_ref[t_tile:t_tile + CONV_HALO, :]

    ca = ca_ref[...].astype(F32)
    cb = cb_ref[...].astype(F32)
    ext_ref[CONV_HALO:, :] = ca * _sigmoid(cb)

    first_tap = CONV_HALO - (CONV_WIDTH - 1)
    bias = b_ref[...]
    lg = lg_ref[...]
    lb = lb_ref[...]
    for r0 in range(0, t_tile, r_chunk):
        acc = jnp.zeros((r_chunk, CONV_CH), F32) + bias
        for j in range(CONV_WIDTH):
            acc = acc + ext_ref[pl.ds(r0 + first_tap + j, r_chunk), :] * w_ref[j:j + 1, :]
        mu = jnp.mean(acc, axis=-1, keepdims=True)
        yc = acc - mu
        var = jnp.mean(yc * yc, axis=-1, keepdims=True)
        y = yc * lax.rsqrt(var + NORM_EPS) * lg + lb
        o_ref[r0:r0 + r_chunk, :] = (y * _sigmoid(y)).astype(o_ref.dtype)


def _conv_branch(proj, conv_w, conv_b, ln_g, ln_b, batch, seq):
    n = proj.shape[0]
    t_tile = 512
    tiles = seq // t_tile
    w = jnp.zeros((CONV_HALO, CONV_CH), F32).at[:CONV_WIDTH].set(conv_w)
    vec = pl.BlockSpec((1, CONV_CH), lambda b, t: (0, 0))

    def col(off):
        return pl.BlockSpec((t_tile, CONV_CH), lambda b, t, off=off: (b * tiles + t, off // CONV_CH))

    return pl.pallas_call(
        functools.partial(_conv_kernel, t_tile=t_tile, r_chunk=64),
        grid=(batch, tiles),
        in_specs=[col(OFF_CA), col(OFF_CB), pl.BlockSpec((CONV_HALO, CONV_CH), lambda b, t: (0, 0)),
                  vec, vec, vec],
        out_specs=pl.BlockSpec((t_tile, CONV_CH), lambda b, t: (b * tiles + t, 0)),
        out_shape=jax.ShapeDtypeStruct((n, CONV_CH), BF16),
        scratch_shapes=[pltpu.VMEM((CONV_HALO + t_tile, CONV_CH), F32)],
        compiler_params=_cparams("parallel", "arbitrary"),
        name="conv_branch",
    )(proj, proj, w, conv_b.reshape(1, -1), ln_g.reshape(1, -1), ln_b.reshape(1, -1))


def _moba_kernel(q_ref, k_ref, v_ref, qg_ref, kg_ref, cos_ref, sin_ref, o_ref,
                 qf_s, q0_s, q1_s, k0_s, k1_s, km_s, *, n_blocks):
    bs = MOBA_BLOCK
    hd = MOBA_HEAD_DIM
    lane = lax.broadcasted_iota(jnp.int32, (1, LANES), 1)
    head0 = lane < hd
    first_half = (lane % hd) < (hd // 2)
    q_scale = (hd ** -0.5) * LOG2_E
    qg = qg_ref[...]
    kg = kg_ref[...]

    def norm_rope(x, g, cs, sn):
        x2 = x * x
        s0 = jnp.sum(jnp.where(head0, x2, 0.0), axis=-1, keepdims=True)
        s1 = jnp.sum(jnp.where(head0, 0.0, x2), axis=-1, keepdims=True)
        ms = jnp.where(head0, s0, s1) * (1.0 / hd)
        y = x * lax.rsqrt(ms + NORM_EPS) * g
        partner = jnp.where(first_half, pltpu.roll(y, LANES - hd // 2, 1), pltpu.roll(y, hd // 2, 1))
        return y * cs + partner * sn

    km_s[...] = jnp.zeros(km_s.shape, F32)

    def prologue(j, carry):
        r = pl.ds(pl.multiple_of(j * bs, bs), bs)
        cs = cos_ref[r, :]
        sn = sin_ref[r, :]
        qn = norm_rope(q_ref[r, :].astype(F32), qg, cs, sn)
        kn = norm_rope(k_ref[r, :].astype(F32), kg, cs, sn)
        qf_s[r, :] = qn
        qsc = qn * q_scale
        q0_s[r, :] = jnp.where(head0, qsc, 0.0).astype(BF16)
        q1_s[r, :] = jnp.where(head0, 0.0, qsc).astype(BF16)
        k0_s[r, :] = jnp.where(head0, kn, jnp.where(lane == hd + j, 1.0, 0.0)).astype(BF16)
        k1_s[r, :] = jnp.where(head0, jnp.where(lane == j, 1.0, 0.0), kn).astype(BF16)
        kmean = jnp.sum(kn, axis=0, keepdims=True) * (1.0 / bs)
        km_s[pl.ds(hd + j, 1), :] = jnp.where(head0, kmean, 0.0)
        km_s[pl.ds(j, 1), :] = jnp.where(head0, 0.0, kmean)
        return carry

    lax.fori_loop(0, n_blocks, prologue, 0)

    causal = (lax.broadcasted_iota(jnp.int32, (bs, bs), 0) >= lax.broadcasted_iota(jnp.int32, (bs, bs), 1))
    sub = lax.broadcasted_iota(jnp.int32, (8, bs), 0)

    def bias_rows(grp, b):
        rank = jnp.zeros(grp.shape, F32)
        for jp in range(b):
            row = grp[jp:jp + 1, :]
            ahead = (row > grp) | ((row == grp) & (jp < sub))
            rank = rank + jnp.where(ahead, 1.0, 0.0)
        return jnp.where((rank < float(MOBA_TOPK)) | (sub >= b), 0.0, NEG_BIG)

    for b in range(n_blocks):
        rq = slice(b * bs, (b + 1) * bs)
        past = b * bs
        if b > MOBA_TOPK:
            gate_t = lax.dot_general(km_s[...], qf_s[rq, :], _NT, precision=lax.Precision.HIGHEST,
                                     preferred_element_type=F32)
            pad = jnp.zeros((hd - 8, bs), F32)
            bias_t = jnp.concatenate([bias_rows(gate_t[0:8, :], b), pad,
                                      bias_rows(gate_t[hd:hd + 8, :], b), pad], axis=0)
            bias = bias_t.T
            qsc = qf_s[rq, :] * q_scale
            q_aug = (jnp.where(head0, qsc, bias).astype(BF16), jnp.where(head0, bias, qsc).astype(BF16))
        else:
            q_aug = (q0_s[rq, :], q1_s[rq, :])
        outs = []
        for h in range(2):
            ks = (k0_s, k1_s)[h]
            s_d = lax.dot_general(q_aug[h], ks[rq, :], _NT, preferred_element_type=F32)
            s_d = jnp.where(causal, s_d, NEG_BIG)
            m = jnp.max(s_d, axis=-1, keepdims=True)
            if past:
                s_p = lax.dot_general(q_aug[h], ks[0:past, :], _NT, preferred_element_type=F32)
                m = jnp.maximum(m, jnp.max(s_p, axis=-1, keepdims=True))
            p_d = jnp.exp2(s_d - m)
            l = jnp.sum(p_d, axis=-1, keepdims=True)
            acc = jnp.dot(p_d.astype(BF16), v_ref[rq, :], preferred_element_type=F32)
            if past:
                p_p = jnp.exp2(s_p - m)
                l = l + jnp.sum(p_p, axis=-1, keepdims=True)
                acc = acc + jnp.dot(p_p.astype(BF16), v_ref[0:past, :], preferred_element_type=F32)
            outs.append(acc / l)
        o_ref[rq, :] = jnp.where(head0, outs[0], outs[1]).astype(o_ref.dtype)


def _moba(proj, q_gain, k_gain, tabs, batch, seq):
    n = proj.shape[0]
    assert seq % MOBA_BLOCK == 0
    n_blocks = seq // MOBA_BLOCK
    assert n_blocks <= 8
    pairs = MOBA_W // LANES

    def col(off):
        return pl.BlockSpec((seq, LANES), lambda b, p, off=off: (b, off // LANES + p))

    tab = pl.BlockSpec((seq, LANES), lambda b, p: (0, 0))
    vec = pl.BlockSpec((1, LANES), lambda b, p: (0, 0))
    return pl.pallas_call(
        functools.partial(_moba_kernel, n_blocks=n_blocks),
        grid=(batch, pairs),
        in_specs=[col(OFF_MQ), col(OFF_MK), col(OFF_MV), vec, vec, tab, tab],
        out_specs=pl.BlockSpec((seq, LANES), lambda b, p: (b, p)),
        out_shape=jax.ShapeDtypeStruct((n, MOBA_W), BF16),
        scratch_shapes=[pltpu.VMEM((seq, LANES), F32), pltpu.VMEM((seq, LANES), BF16),
                        pltpu.VMEM((seq, LANES), BF16), pltpu.VMEM((seq, LANES), BF16),
                        pltpu.VMEM((seq, LANES), BF16), pltpu.VMEM((LANES, LANES), F32)],
        compiler_params=_cparams("parallel", "parallel"),
        name="moba",
    )(proj, proj, proj, jnp.tile(q_gain, 2).reshape(1, LANES), jnp.tile(k_gain, 2).reshape(1, LANES),
      tabs["moba_cos"], tabs["moba_sin"])


def _merge_kernel(x_ref, ga_ref, gb_ref, yr_ref, yc_ref, ym_ref, wr_ref, wc_ref, wm_ref, wo_ref, gf_ref,
                  *rest, with_router):
    if with_router:
        wrt_ref, xo_ref, h_ref, lg_ref = rest
    else:
        xo_ref, h_ref = rest
    d = D_MODEL
    ga = ga_ref[...].astype(F32)
    gb = gb_ref[...].astype(F32)
    g_ret = ga[:, :d]
    g_conv = jnp.concatenate([ga[:, d:], gb[:, :d // 2]], axis=-1)
    g_moba = gb[:, d // 2:]
    merged = (_sigmoid(g_ret) * jnp.dot(yr_ref[...], wr_ref[...], preferred_element_type=F32)
              + _sigmoid(g_conv) * jnp.dot(yc_ref[...], wc_ref[...], preferred_element_type=F32)
              + _sigmoid(g_moba) * jnp.dot(ym_ref[...], wm_ref[...], preferred_element_type=F32))
    x = x_ref[...] + jnp.dot(merged.astype(BF16), wo_ref[...], preferred_element_type=F32)
    xo_ref[...] = x
    ms = jnp.mean(x * x, axis=-1, keepdims=True)
    h = x * lax.rsqrt(ms + NORM_EPS) * gf_ref[...]
    if with_router:
        h_ref[...] = _pack_bf16_pairs(h)
        lg_ref[...] = jnp.dot(h, wrt_ref[...], precision=lax.Precision.HIGHEST, preferred_element_type=F32)
    else:
        h_ref[...] = h.astype(BF16)


def _merge(xf, proj, yr, yc, ym, w_ret_o, w_conv_o, w_moba_o, w_out, g_ffn, w_router):
    n, d = xf.shape
    tm = 512
    gw = 3 * d // 2
    with_router = w_router is not None
    row = lambda w: pl.BlockSpec((tm, w), lambda i: (i, 0))
    full = lambda a: pl.BlockSpec(a.shape, lambda i: (0,) * a.ndim)
    wr, wc, wm, wo = (w.astype(BF16) for w in (w_ret_o, w_conv_o, w_moba_o, w_out))
    gf = g_ffn.reshape(1, d)
    args = [xf, proj, proj, yr, yc, ym, wr, wc, wm, wo, gf]
    in_specs = [row(d),
                pl.BlockSpec((tm, gw), lambda i: (i, OFF_GATES // gw)),
                pl.BlockSpec((tm, gw), lambda i: (i, OFF_GATES // gw + 1)),
                row(RET_W), row(CONV_CH), row(MOBA_W), full(wr), full(wc), full(wm), full(wo), full(gf)]
    if with_router:
        out_shape = [jax.ShapeDtypeStruct((n, d), F32), jax.ShapeDtypeStruct((n, d // 2), jnp.uint32)]
        out_specs = [row(d), row(d // 2)]
    else:
        out_shape = [jax.ShapeDtypeStruct((n, d), F32), jax.ShapeDtypeStruct((n, d), BF16)]
        out_specs = [row(d), row(d)]
    if with_router:
        wrt = jnp.zeros((d, LANES), F32).at[:, :N_EXPERTS].set(w_router)
        args.append(wrt)
        in_specs.append(full(wrt))
        out_shape.append(jax.ShapeDtypeStruct((n, LANES), F32))
        out_specs.append(row(LANES))
    return pl.pallas_call(
        functools.partial(_merge_kernel, with_router=with_router),
        grid=(n // tm,),
        in_specs=in_specs, out_specs=out_specs, out_shape=out_shape,
        compiler_params=_cparams("parallel"),
        name="merge",
    )(*args)


def _ffn_kernel(x_ref, h_ref, wg_ref, wu_ref, wd_ref, o_ref):
    f = pl.program_id(1)
    h = h_ref[...]
    g = jnp.dot(h, wg_ref[...], preferred_element_type=F32)
    u = jnp.dot(h, wu_ref[...], preferred_element_type=F32)
    a = (g * _sigmoid(g) * u).astype(BF16)
    dlt = jnp.dot(a, wd_ref[...], preferred_element_type=F32)

    @pl.when(f == 0)
    def _():
        o_ref[...] = x_ref[...] + dlt

    @pl.when(f > 0)
    def _():
        o_ref[...] += dlt


def _dense_ffn(xf, h, w_gate, w_up, w_down):
    n, d = xf.shape
    ff = w_gate.shape[1]
    tm, tf = 512, 1408
    assert ff % tf == 0
    return pl.pallas_call(
        _ffn_kernel,
        grid=(n // tm, ff // tf),
        in_specs=[pl.BlockSpec((tm, d), lambda i, f: (i, 0)),
                  pl.BlockSpec((tm, d), lambda i, f: (i, 0)),
                  pl.BlockSpec((d, tf), lambda i, f: (0, f)),
                  pl.BlockSpec((d, tf), lambda i, f: (0, f)),
                  pl.BlockSpec((tf, d), lambda i, f: (f, 0))],
        out_specs=pl.BlockSpec((tm, d), lambda i, f: (i, 0)),
        out_shape=jax.ShapeDtypeStruct((n, d), F32),
        compiler_params=_cparams("parallel", "arbitrary"),
        name="dense_ffn",
    )(xf, h, w_gate.astype(BF16), w_up.astype(BF16), w_down.astype(BF16))


def _row_copies(n_rows, make_copy):
    def start(i, carry):
        make_copy(i).start()
        return carry

    def wait(i, carry):
        make_copy(0).wait()
        return carry

    lax.fori_loop(0, n_rows, start, 0, unroll=8)
    lax.fori_loop(0, n_rows, wait, 0, unroll=8)


def _dispatch_kernel(dest_ref, hp_ref, xz_ref, xbuf_ref, sem, *, n_tok):
    del xz_ref

    def copy(a):
        tok = lax.shift_right_logical(a, 1)
        return pltpu.make_async_copy(hp_ref.at[pl.ds(tok, 1), :], xbuf_ref.at[pl.ds(dest_ref[0, 0, a], 1), :], sem)

    _row_copies(TOP_K * n_tok, copy)


def _dispatch(hp, dest3, p_rows):
    n, w = hp.shape
    steps, _, per_step = dest3.shape
    n_tok = per_step // TOP_K
    return pl.pallas_call(
        functools.partial(_dispatch_kernel, n_tok=n_tok),
        grid=(steps,),
        in_specs=[pl.BlockSpec((1, 1, per_step), lambda i: (i, 0, 0), memory_space=pltpu.SMEM),
                  pl.BlockSpec((n_tok, w), lambda i: (i, 0)),
                  pl.BlockSpec(memory_space=pl.ANY)],
        out_specs=pl.BlockSpec(memory_space=pl.ANY),
        out_shape=jax.ShapeDtypeStruct((p_rows, w), hp.dtype),
        scratch_shapes=[pltpu.SemaphoreType.DMA(())],
        input_output_aliases={2: 0},
        compiler_params=_cparams("arbitrary"),
        name="moe_dispatch",
    )(dest3, hp, jnp.zeros((p_rows, w), hp.dtype))


def _moe_kernel(te_ref, act_ref, x_ref, wg_ref, wu_ref, wd_ref, o_ref, xs_ref):
    i = pl.program_id(0)
    f = pl.program_id(1)
    active = act_ref[i] > 0
    half = xs_ref.shape[1] // 2

    @pl.when(active & (f == 0))
    def _():
        left, right = _unpack_bf16_pairs(x_ref[...])
        xs_ref[:, :half] = left
        xs_ref[:, half:] = right

    @pl.when(active)
    def _():
        x = xs_ref[...]
        g = jnp.dot(x, wg_ref[0], preferred_element_type=F32)
        u = jnp.dot(x, wu_ref[0], preferred_element_type=F32)
        a = (g * _sigmoid(g) * u).astype(BF16)
        dlt = jnp.dot(a, wd_ref[0], preferred_element_type=F32)

        @pl.when(f == 0)
        def _():
            o_ref[...] = dlt

        @pl.when(f > 0)
        def _():
            o_ref[...] += dlt

    @pl.when(jnp.logical_not(active) & (f == 0))
    def _():
        o_ref[...] = jnp.zeros(o_ref.shape, o_ref.dtype)


def _moe_experts(xbuf, tile_expert, tile_active, w_gate, w_up, w_down, tm):
    p, half = xbuf.shape
    d = 2 * half
    ff = w_gate.shape[2]
    tf = 896
    assert ff % tf == 0
    nf = ff // tf

    def fsel(f, act, i):
        return jnp.where(act[i] > 0, f, nf - 1)

    grid_spec = pltpu.PrefetchScalarGridSpec(
        num_scalar_prefetch=2,
        grid=(p // tm, nf),
        in_specs=[pl.BlockSpec((tm, half), lambda i, f, te, act: (i, 0)),
                  pl.BlockSpec((1, d, tf), lambda i, f, te, act: (te[i], 0, fsel(f, act, i))),
                  pl.BlockSpec((1, d, tf), lambda i, f, te, act: (te[i], 0, fsel(f, act, i))),
                  pl.BlockSpec((1, tf, d), lambda i, f, te, act: (te[i], fsel(f, act, i), 0))],
        out_specs=pl.BlockSpec((tm, d), lambda i, f, te, act: (i, 0)),
        scratch_shapes=[pltpu.VMEM((tm, d), BF16)],
    )
    return pl.pallas_call(
        _moe_kernel,
        grid_spec=grid_spec,
        out_shape=jax.ShapeDtypeStruct((p, d), F32),
        compiler_params=_cparams("arbitrary", "arbitrary"),
        name="moe_experts",
    )(tile_expert, tile_active, xbuf, w_gate.astype(BF16), w_up.astype(BF16), w_down.astype(BF16))


def _combine_kernel(dest_ref, x_ref, g_ref, ybuf_ref, o_ref, ys_ref, sem, *, n_tok):
    def copy(a):
        slot = (a & 1) * n_tok + lax.shift_right_logical(a, 1)
        return pltpu.make_async_copy(ybuf_ref.at[pl.ds(dest_ref[0, 0, a], 1), :], ys_ref.at[pl.ds(slot, 1), :], sem)

    _row_copies(TOP_K * n_tok, copy)
    g = g_ref[...]
    y = g[:, 0:1] * ys_ref[0:n_tok, :] + g[:, 1:2] * ys_ref[n_tok:, :]
    o_ref[...] = x_ref[...] + y


def _combine(xf, gates, dest3, ybuf):
    n, d = xf.shape
    steps, _, per_step = dest3.shape
    n_tok = per_step // TOP_K
    assert TOP_K == 2
    return pl.pallas_call(
        functools.partial(_combine_kernel, n_tok=n_tok),
        grid=(steps,),
        in_specs=[pl.BlockSpec((1, 1, per_step), lambda i: (i, 0, 0), memory_space=pltpu.SMEM),
                  pl.BlockSpec((n_tok, d), lambda i: (i, 0)),
                  pl.BlockSpec((n_tok, TOP_K), lambda i: (i, 0)),
                  pl.BlockSpec(memory_space=pl.ANY)],
        out_specs=pl.BlockSpec((n_tok, d), lambda i: (i, 0)),
        out_shape=jax.ShapeDtypeStruct((n, d), F32),
        scratch_shapes=[pltpu.VMEM((TOP_K * n_tok, d), F32), pltpu.SemaphoreType.DMA(())],
        compiler_params=_cparams("arbitrary"),
        name="moe_combine",
    )(dest3, xf, gates, ybuf)


def _moe(xf, hp, logits, w_gate, w_up, w_down):
    n, d = xf.shape
    na = n * TOP_K
    tm = 512
    n_tok = 512
    n_tiles = na // tm + N_EXPERTS
    top_vals, top_idx = lax.top_k(logits, TOP_K)
    gates = jax.nn.softmax(top_vals, axis=-1)
    e_flat = top_idx.reshape(-1)
    onehot = (e_flat[:, None] == jnp.arange(N_EXPERTS, dtype=e_flat.dtype)[None, :]).astype(jnp.int32)
    csum = jnp.cumsum(onehot, axis=0)
    pos_in_e = jnp.sum(csum * onehot, axis=1) - 1
    counts = csum[-1]
    tiles_per_e = (counts + tm - 1) // tm
    tile_end = jnp.cumsum(tiles_per_e)
    tile_start = tile_end - tiles_per_e
    dest = jnp.sum(tile_start[None, :] * onehot, axis=1) * tm + pos_in_e
    tile_ids = jnp.arange(n_tiles, dtype=jnp.int32)
    tile_expert = jnp.minimum(jnp.sum((tile_end[None, :] <= tile_ids[:, None]).astype(jnp.int32), axis=1),
                              N_EXPERTS - 1).astype(jnp.int32)
    tile_active = (tile_ids < tile_end[-1]).astype(jnp.int32)
    dest3 = dest.astype(jnp.int32).reshape(n // n_tok, 1, TOP_K * n_tok)
    xbuf = _dispatch(hp, dest3, n_tiles * tm)
    ybuf = _moe_experts(xbuf, tile_expert, tile_active, w_gate, w_up, w_down, tm)
    return _combine(xf, gates, dest3, ybuf)


def _rope(seq, dim):
    inv = 1.0 / (ROPE_THETA ** (jnp.arange(0, dim, 2, dtype=F32) / dim))
    ang = jnp.arange(seq, dtype=F32)[:, None] * inv[None, :]
    return jnp.cos(ang), jnp.sin(ang)


def _tables(seq):
    cos_r, sin_r = _rope(seq, RET_HEAD_DIM)
    cos_m, sin_m = _rope(seq, MOBA_HEAD_DIM)
    c_len = RET_CHUNK
    log_g = jnp.log(1.0 - jnp.exp2(-5.0 - jnp.arange(RET_HEADS, dtype=F32)))
    pos = jnp.arange(c_len, dtype=F32)
    diff = pos[:, None] - pos[None, :]
    decay = jnp.where(diff[None] >= 0, jnp.exp(diff[None] * log_g[:, None, None]), 0.0)
    w_state = jnp.exp((c_len - 1.0 - pos)[None, :] * log_g[:, None])
    w_cross = jnp.exp((pos + 1.0)[None, :] * log_g[:, None])
    g_chunk = jnp.exp(c_len * log_g)
    bcast = lambda a: jnp.broadcast_to(a[:, :, None], (RET_HEADS, c_len, RET_HEAD_DIM))
    return {
        "ret_cos": jnp.concatenate([cos_r, cos_r], axis=-1),
        "ret_sin": jnp.concatenate([-sin_r, sin_r], axis=-1),
        "ret_decay": decay,
        "ret_wstate": bcast(w_state),
        "ret_wcross": bcast(w_cross),
        "ret_gchunk": jnp.broadcast_to(g_chunk[:, None, None], (RET_HEADS, RET_HEAD_DIM, RET_HEAD_DIM)),
        "moba_cos": jnp.tile(cos_m, (1, 2 * LANES // MOBA_HEAD_DIM)),
        "moba_sin": jnp.tile(jnp.concatenate([-sin_m, sin_m], axis=-1), (1, LANES // MOBA_HEAD_DIM)),
    }


def kernel(x, g_mix, w_in, conv_w, conv_b, conv_ln_g, conv_ln_b, q_norm_g, k_norm_g, w_ret_o, w_conv_o,
           w_moba_o, w_out, g_ffn, w_ff_gate, w_ff_up, w_ff_down, w_router, w_e_gate, w_e_up, w_e_down):
    batch, seq, d = x.shape
    depth = g_mix.shape[0]
    assert d == D_MODEL and w_in.shape[2] == IN_COLS
    xf = x.reshape(batch * seq, d)
    tabs = _tables(seq)
    for l in range(depth):
        proj = _inproj(xf, g_mix[l], w_in[l].astype(BF16))
        yr = _retention(proj, tabs, batch, seq)
        yc = _conv_branch(proj, conv_w[l], conv_b[l], conv_ln_g[l], conv_ln_b[l], batch, seq)
        ym = _moba(proj, q_norm_g[l], k_norm_g[l], tabs, batch, seq)
        j = l // 2
        if l % 2 == 0:
            xf, h = _merge(xf, proj, yr, yc, ym, w_ret_o[l], w_conv_o[l], w_moba_o[l], w_out[l], g_ffn[l], None)
            xf = _dense_ffn(xf, h, w_ff_gate[j], w_ff_up[j], w_ff_down[j])
        else:
            xf, h, logits = _merge(xf, proj, yr, yc, ym, w_ret_o[l], w_conv_o[l], w_moba_o[l], w_out[l],
                                   g_ffn[l], w_router[j])
            xf = _moe(xf, h, logits[:, :N_EXPERTS], w_e_gate[j], w_e_up[j], w_e_down[j])
    return xf.reshape(batch, seq, d)
```

```python
import functools

import jax
import jax.numpy as jnp
from jax import lax
from jax.experimental import pallas as pl
from jax.experimental.pallas import tpu as pltpu

F32 = jnp.float32
BF16 = jnp.bfloat16

D_MODEL = 1024
RET_HEADS = 4
RET_HEAD_DIM = 128
RET_W = RET_HEADS * RET_HEAD_DIM
RET_CHUNK = 128
CONV_CH = 512
CONV_WIDTH = 31
MOBA_HEADS = 8
MOBA_HEAD_DIM = 64
MOBA_W = MOBA_HEADS * MOBA_HEAD_DIM
MOBA_BLOCK = 256
MOBA_TOPK = 3
ROPE_THETA = 10000.0
N_EXPERTS = 8
TOP_K = 2
NORM_EPS = 1e-6
GROUP_NORM_EPS = 1e-5

OFF_RQ, OFF_RK, OFF_RV, OFF_RG = 0, 512, 1024, 1536
OFF_CA, OFF_CB = 2048, 2560
OFF_MQ, OFF_MK, OFF_MV = 3072, 3584, 4096
OFF_GATES = 4608
IN_COLS = OFF_GATES + 3 * D_MODEL

LANES = 128
SUBLANES = 8
CONV_HALO = 32
VMEM_LIMIT_BYTES = 56 * 1024 * 1024
NEG_BIG = -1e30
LOG2_E = 1.4426950408889634

_NT = (((1,), (1,)), ((), ()))
_TN = (((0,), (0,)), ((), ()))


def _cparams(*sem):
    return pltpu.CompilerParams(dimension_semantics=sem, vmem_limit_bytes=VMEM_LIMIT_BYTES)


def _sigmoid(x):
    return 1.0 / (1.0 + jnp.exp(-x))


def _pack_bf16_pairs(x):
    w = x.shape[1] // 2
    bits = pltpu.bitcast(x.astype(BF16).astype(F32), jnp.uint32)
    return lax.shift_right_logical(bits[:, :w], jnp.uint32(16)) | bits[:, w:]


def _unpack_bf16_pairs(words):
    left = pltpu.bitcast(lax.shift_left(words, jnp.uint32(16)), F32)
    right = pltpu.bitcast(words & jnp.uint32(0xFFFF0000), F32)
    return left.astype(BF16), right.astype(BF16)


def _rms_norm(x, gain):
    ms = jnp.mean(x * x, axis=-1, keepdims=True)
    return x * lax.rsqrt(ms + NORM_EPS) * gain


def _rmsnorm_kernel(x_ref, g_ref, o_ref):
    o_ref[...] = _rms_norm(x_ref[...], g_ref[...]).astype(o_ref.dtype)


def _rmsnorm(xf, gain):
    n, d = xf.shape
    tm = 1024
    return pl.pallas_call(
        _rmsnorm_kernel,
        grid=(n // tm,),
        in_specs=[pl.BlockSpec((tm, d), lambda i: (i, 0)), pl.BlockSpec((1, d), lambda i: (0, 0))],
        out_specs=pl.BlockSpec((tm, d), lambda i: (i, 0)),
        out_shape=jax.ShapeDtypeStruct((n, d), BF16),
        compiler_params=_cparams("parallel"),
        name="rmsnorm",
    )(xf, gain.reshape(1, d))


def _inproj_kernel(h_ref, w_ref, o_ref, wb_ref):
    @pl.when(pl.program_id(1) == 0)
    def _():
        wb_ref[...] = w_ref[0].astype(BF16)

    o_ref[...] = jnp.dot(h_ref[...], wb_ref[...], preferred_element_type=F32).astype(o_ref.dtype)


def _inproj(h, w_in, layer):
    n, d = h.shape
    c = w_in.shape[2]
    tm, tn = 1024, 1536
    return pl.pallas_call(
        _inproj_kernel,
        grid=(c // tn, n // tm),
        in_specs=[pl.BlockSpec((tm, d), lambda j, i: (i, 0)),
                  pl.BlockSpec((1, d, tn), lambda j, i: (layer, 0, j))],
        out_specs=pl.BlockSpec((tm, tn), lambda j, i: (i, j)),
        out_shape=jax.ShapeDtypeStruct((n, c), BF16),
        scratch_shapes=[pltpu.VMEM((d, tn), BF16)],
        compiler_params=_cparams("arbitrary", "arbitrary"),
        name="inproj",
    )(h, w_in)


def _ret_kernel(q_ref, k_ref, v_ref, g_ref, cos_ref, sin_ref, dec_ref, wst_ref, wcr_ref, gch_ref,
                o_ref, state_ref, *, n_chunks):
    c_len = RET_CHUNK
    hd = RET_HEAD_DIM
    state_ref[...] = jnp.zeros(state_ref.shape, F32)

    def body(c, carry):
        r = pl.ds(pl.multiple_of(c * c_len, c_len), c_len)
        cs = cos_ref[r, :]
        sn = sin_ref[r, :]
        for h in range(RET_HEADS):
            hs = slice(h * hd, (h + 1) * hd)
            q = q_ref[r, hs].astype(F32)
            k = k_ref[r, hs].astype(F32)
            q = q * cs + pltpu.roll(q, hd // 2, 1) * sn
            k = (k * cs + pltpu.roll(k, hd // 2, 1) * sn) * (hd ** -0.5)
            qb = q.astype(BF16)
            kb = k.astype(BF16)
            vb = v_ref[r, hs]
            state = state_ref[h]
            s = lax.dot_general(qb, kb, _NT, preferred_element_type=F32) * dec_ref[h]
            y = jnp.dot(s.astype(BF16), vb, preferred_element_type=F32)
            y = y + jnp.dot(qb, state.astype(BF16), preferred_element_type=F32) * wcr_ref[h]
            kw = (k * wst_ref[h]).astype(BF16)
            kv = lax.dot_general(kw, vb, _TN, preferred_element_type=F32)
            state_ref[h] = gch_ref[h] * state + kv
            mu = jnp.mean(y, axis=-1, keepdims=True)
            yc = y - mu
            var = jnp.mean(yc * yc, axis=-1, keepdims=True)
            yn = yc * lax.rsqrt(var + GROUP_NORM_EPS)
            g = g_ref[r, hs].astype(F32)
            o_ref[r, hs] = (g * _sigmoid(g) * yn).astype(o_ref.dtype)
        return carry

    lax.fori_loop(0, n_chunks, body, 0)


def _retention(proj, tabs, batch, seq):
    n = proj.shape[0]
    hd = RET_HEAD_DIM

    def col(off):
        return pl.BlockSpec((seq, RET_W), lambda b, off=off: (b, off // RET_W))

    tab = pl.BlockSpec((seq, hd), lambda b: (0, 0))
    htab = pl.BlockSpec((RET_HEADS, hd, hd), lambda b: (0, 0, 0))
    return pl.pallas_call(
        functools.partial(_ret_kernel, n_chunks=seq // RET_CHUNK),
        grid=(batch,),
        in_specs=[col(OFF_RQ), col(OFF_RK), col(OFF_RV), col(OFF_RG), tab, tab, htab, htab, htab, htab],
        out_specs=pl.BlockSpec((seq, RET_W), lambda b: (b, 0)),
        out_shape=jax.ShapeDtypeStruct((n, RET_W), BF16),
        scratch_shapes=[pltpu.VMEM((RET_HEADS, hd, hd), F32)],
        compiler_params=_cparams("parallel"),
        name="retention",
    )(proj, proj, proj, proj, tabs["ret_cos"], tabs["ret_sin"], tabs["ret_decay"], tabs["ret_wstate"],
      tabs["ret_wcross"], tabs["ret_gchunk"])


def _conv_kernel(ca_ref, cb_ref, w_ref, b_ref, lg_ref, lb_ref, o_ref, ext_ref, sh_ref, *, t_tile, r_chunk):
    t = pl.program_id(1)

    @pl.when(t == 0)
    def _():
        ext_ref[0:CONV_HALO, :] = jnp.zeros((CONV_HALO, CONV_CH), F32)

    @pl.when(t > 0)
    def _():
        ext_ref[0:CONV_HALO, :] = ext_ref[t_tile:t_tile + CONV_HALO, :]

    ca = ca_ref[...].astype(F32)
    cb = cb_ref[...].astype(F32)
    ext_ref[CONV_HALO:, :] = ca * _sigmoid(cb)

    first_tap = CONV_HALO - (CONV_WIDTH - 1)
    shifted_rows = sh_ref.shape[1]
    for s in range(1, SUBLANES):
        sh_ref[s] = ext_ref[s:s + shifted_rows, :]
    bias = b_ref[...]
    lg = lg_ref[...]
    lb = lb_ref[...]
    for r0 in range(0, t_tile, r_chunk):
        acc = jnp.zeros((r_chunk, CONV_CH), F32) + bias
        for j in range(CONV_WIDTH):
            off = first_tap + j
            s = off % SUBLANES
            rows = pl.ds(r0 + off - s, r_chunk)
            tap = ext_ref[rows, :] if s == 0 else sh_ref[s, rows, :]
            acc = acc + tap * w_ref[j:j + 1, :]
        mu = jnp.mean(acc, axis=-1, keepdims=True)
        yc = acc - mu
        var = jnp.mean(yc * yc, axis=-1, keepdims=True)
        y = yc * lax.rsqrt(var + NORM_EPS) * lg + lb
        o_ref[r0:r0 + r_chunk, :] = (y * _sigmoid(y)).astype(o_ref.dtype)


def _conv_branch(proj, conv_w, conv_b, ln_g, ln_b, batch, seq):
    n = proj.shape[0]
    t_tile = 512
    tiles = seq // t_tile
    w = jnp.zeros((CONV_HALO, CONV_CH), F32).at[:CONV_WIDTH].set(conv_w)
    vec = pl.BlockSpec((1, CONV_CH), lambda b, t: (0, 0))

    def col(off):
        return pl.BlockSpec((t_tile, CONV_CH), lambda b, t, off=off: (b * tiles + t, off // CONV_CH))

    return pl.pallas_call(
        functools.partial(_conv_kernel, t_tile=t_tile, r_chunk=64),
        grid=(batch, tiles),
        in_specs=[col(OFF_CA), col(OFF_CB), pl.BlockSpec((CONV_HALO, CONV_CH), lambda b, t: (0, 0)),
                  vec, vec, vec],
        out_specs=pl.BlockSpec((t_tile, CONV_CH), lambda b, t: (b * tiles + t, 0)),
        out_shape=jax.ShapeDtypeStruct((n, CONV_CH), BF16),
        scratch_shapes=[pltpu.VMEM((CONV_HALO + t_tile, CONV_CH), F32),
                        pltpu.VMEM((SUBLANES, CONV_HALO + t_tile - SUBLANES, CONV_CH), F32)],
        compiler_params=_cparams("parallel", "arbitrary"),
        name="conv_branch",
    )(proj, proj, w, conv_b.reshape(1, -1), ln_g.reshape(1, -1), ln_b.reshape(1, -1))


def _moba_kernel(q_ref, k_ref, v_ref, qg_ref, kg_ref, cos_ref, sin_ref, o_ref,
                 qf_s, q0_s, q1_s, k0_s, k1_s, km_s, *, n_blocks):
    bs = MOBA_BLOCK
    hd = MOBA_HEAD_DIM
    lane = lax.broadcasted_iota(jnp.int32, (1, LANES), 1)
    head0 = lane < hd
    first_half = (lane % hd) < (hd // 2)
    q_scale = (hd ** -0.5) * LOG2_E
    qg = qg_ref[...]
    kg = kg_ref[...]

    def norm_rope(x, g, cs, sn):
        x2 = x * x
        s0 = jnp.sum(jnp.where(head0, x2, 0.0), axis=-1, keepdims=True)
        s1 = jnp.sum(jnp.where(head0, 0.0, x2), axis=-1, keepdims=True)
        ms = jnp.where(head0, s0, s1) * (1.0 / hd)
        y = x * lax.rsqrt(ms + NORM_EPS) * g
        partner = jnp.where(first_half, pltpu.roll(y, LANES - hd // 2, 1), pltpu.roll(y, hd // 2, 1))
        return y * cs + partner * sn

    km_s[...] = jnp.zeros(km_s.shape, F32)

    def prologue(j, carry):
        r = pl.ds(pl.multiple_of(j * bs, bs), bs)
        cs = cos_ref[r, :]
        sn = sin_ref[r, :]
        qn = norm_rope(q_ref[r, :].astype(F32), qg, cs, sn)
        kn = norm_rope(k_ref[r, :].astype(F32), kg, cs, sn)
        qf_s[r, :] = qn
        qsc = qn * q_scale
        q0_s[r, :] = jnp.where(head0, qsc, 0.0).astype(BF16)
        q1_s[r, :] = jnp.where(head0, 0.0, qsc).astype(BF16)
        k0_s[r, :] = jnp.where(head0, kn, jnp.where(lane == hd + j, 1.0, 0.0)).astype(BF16)
        k1_s[r, :] = jnp.where(head0, jnp.where(lane == j, 1.0, 0.0), kn).astype(BF16)
        kmean = jnp.sum(kn, axis=0, keepdims=True) * (1.0 / bs)
        km_s[pl.ds(hd + j, 1), :] = jnp.where(head0, kmean, 0.0)
        km_s[pl.ds(j, 1), :] = jnp.where(head0, 0.0, kmean)
        return carry

    lax.fori_loop(0, n_blocks, prologue, 0)

    causal = (lax.broadcasted_iota(jnp.int32, (bs, bs), 0) >= lax.broadcasted_iota(jnp.int32, (bs, bs), 1))
    sub = lax.broadcasted_iota(jnp.int32, (8, bs), 0)

    def bias_rows(grp, b):
        rank = jnp.zeros(grp.shape, F32)
        for jp in range(b):
            row = grp[jp:jp + 1, :]
            ahead = (row > grp) | ((row == grp) & (jp < sub))
            rank = rank + jnp.where(ahead, 1.0, 0.0)
        return jnp.where((rank < float(MOBA_TOPK)) | (sub >= b), 0.0, NEG_BIG)

    for b in range(n_blocks):
        rq = slice(b * bs, (b + 1) * bs)
        past = b * bs
        if b > MOBA_TOPK:
            gate_t = lax.dot_general(km_s[...], qf_s[rq, :], _NT, precision=lax.Precision.HIGHEST,
                                     preferred_element_type=F32)
            pad = jnp.zeros((hd - 8, bs), F32)
            bias_t = jnp.concatenate([bias_rows(gate_t[0:8, :], b), pad,
                                      bias_rows(gate_t[hd:hd + 8, :], b), pad], axis=0)
            bias = bias_t.T
            qsc = qf_s[rq, :] * q_scale
            q_aug = (jnp.where(head0, qsc, bias).astype(BF16), jnp.where(head0, bias, qsc).astype(BF16))
        else:
            q_aug = (q0_s[rq, :], q1_s[rq, :])
        outs = []
        for h in range(2):
            ks = (k0_s, k1_s)[h]
            s_d = lax.dot_general(q_aug[h], ks[rq, :], _NT, preferred_element_type=F32)
            s_d = jnp.where(causal, s_d, NEG_BIG)
            m = jnp.max(s_d, axis=-1, keepdims=True)
            if past:
                s_p = lax.dot_general(q_aug[h], ks[0:past, :], _NT, preferred_element_type=F32)
                m = jnp.maximum(m, jnp.max(s_p, axis=-1, keepdims=True))
            p_d = jnp.exp2(s_d - m)
            l = jnp.sum(p_d, axis=-1, keepdims=True)
            acc = jnp.dot(p_d.astype(BF16), v_ref[rq, :], preferred_element_type=F32)
            if past:
                p_p = jnp.exp2(s_p - m)
                l = l + jnp.sum(p_p, axis=-1, keepdims=True)
                acc = acc + jnp.dot(p_p.astype(BF16), v_ref[0:past, :], preferred_element_type=F32)
            outs.append(acc / l)
        o_ref[rq, :] = jnp.where(head0, outs[0], outs[1]).astype(o_ref.dtype)


def _moba(proj, q_gain, k_gain, tabs, batch, seq):
    n = proj.shape[0]
    assert seq % MOBA_BLOCK == 0
    n_blocks = seq // MOBA_BLOCK
    assert n_blocks <= 8
    pairs = MOBA_W // LANES

    def col(off):
        return pl.BlockSpec((seq, LANES), lambda b, p, off=off: (b, off // LANES + p))

    tab = pl.BlockSpec((seq, LANES), lambda b, p: (0, 0))
    vec = pl.BlockSpec((1, LANES), lambda b, p: (0, 0))
    return pl.pallas_call(
        functools.partial(_moba_kernel, n_blocks=n_blocks),
        grid=(batch, pairs),
        in_specs=[col(OFF_MQ), col(OFF_MK), col(OFF_MV), vec, vec, tab, tab],
        out_specs=pl.BlockSpec((seq, LANES), lambda b, p: (b, p)),
        out_shape=jax.ShapeDtypeStruct((n, MOBA_W), BF16),
        scratch_shapes=[pltpu.VMEM((seq, LANES), F32), pltpu.VMEM((seq, LANES), BF16),
                        pltpu.VMEM((seq, LANES), BF16), pltpu.VMEM((seq, LANES), BF16),
                        pltpu.VMEM((seq, LANES), BF16), pltpu.VMEM((LANES, LANES), F32)],
        compiler_params=_cparams("parallel", "parallel"),
        name="moba",
    )(proj, proj, proj, jnp.tile(q_gain, 2).reshape(1, LANES), jnp.tile(k_gain, 2).reshape(1, LANES),
      tabs["moba_cos"], tabs["moba_sin"])


def _merge_kernel(x_ref, ga_ref, gb_ref, yr_ref, yc_ref, ym_ref, wr_ref, wc_ref, wm_ref, wo_ref, gf_ref,
                  *rest, with_router):
    if with_router:
        whi_ref, wlo_ref, xo_ref, h_ref, lg_ref = rest
    else:
        xo_ref, h_ref = rest
    d = D_MODEL
    ga = ga_ref[...].astype(F32)
    gb = gb_ref[...].astype(F32)
    g_ret = ga[:, :d]
    g_conv = jnp.concatenate([ga[:, d:], gb[:, :d // 2]], axis=-1)
    g_moba = gb[:, d // 2:]
    merged = (_sigmoid(g_ret) * jnp.dot(yr_ref[...], wr_ref[0], preferred_element_type=F32)
              + _sigmoid(g_conv) * jnp.dot(yc_ref[...], wc_ref[0], preferred_element_type=F32)
              + _sigmoid(g_moba) * jnp.dot(ym_ref[...], wm_ref[0], preferred_element_type=F32))
    x = x_ref[...] + jnp.dot(merged.astype(BF16), wo_ref[0], preferred_element_type=F32)
    xo_ref[...] = x
    h = _rms_norm(x, gf_ref[...])
    if with_router:
        h_ref[...] = _pack_bf16_pairs(h)
        h_hi = h.astype(BF16)
        h_lo = (h - h_hi.astype(F32)).astype(BF16)
        lg_ref[...] = (jnp.dot(h_hi, whi_ref[...], preferred_element_type=F32)
                       + jnp.dot(h_hi, wlo_ref[...], preferred_element_type=F32)
                       + jnp.dot(h_lo, whi_ref[...], preferred_element_type=F32))
    else:
        h_ref[...] = h.astype(BF16)


def _merge(xf, proj, yr, yc, ym, w_ret_o, w_conv_o, w_moba_o, w_out, g_ffn, w_router, layer):
    n, d = xf.shape
    tm = 512
    gw = 3 * d // 2
    with_router = w_router is not None
    row = lambda w: pl.BlockSpec((tm, w), lambda i: (i, 0))
    full = lambda a: pl.BlockSpec(a.shape, lambda i: (0,) * a.ndim)
    of_layer = lambda a: pl.BlockSpec((1,) + a.shape[1:], lambda i: (layer,) + (0,) * (a.ndim - 1))
    gf = g_ffn.reshape(1, d)
    args = [xf, proj, proj, yr, yc, ym, w_ret_o, w_conv_o, w_moba_o, w_out, gf]
    in_specs = [row(d),
                pl.BlockSpec((tm, gw), lambda i: (i, OFF_GATES // gw)),
                pl.BlockSpec((tm, gw), lambda i: (i, OFF_GATES // gw + 1)),
                row(RET_W), row(CONV_CH), row(MOBA_W), of_layer(w_ret_o), of_layer(w_conv_o),
                of_layer(w_moba_o), of_layer(w_out), full(gf)]
    if with_router:
        out_shape = [jax.ShapeDtypeStruct((n, d), F32), jax.ShapeDtypeStruct((n, d // 2), jnp.uint32)]
        out_specs = [row(d), row(d // 2)]
    else:
        out_shape = [jax.ShapeDtypeStruct((n, d), F32), jax.ShapeDtypeStruct((n, d), BF16)]
        out_specs = [row(d), row(d)]
    if with_router:
        wrt = jnp.zeros((d, LANES), F32).at[:, :N_EXPERTS].set(w_router)
        w_hi = wrt.astype(BF16)
        w_lo = (wrt - w_hi.astype(F32)).astype(BF16)
        args += [w_hi, w_lo]
        in_specs += [full(w_hi), full(w_lo)]
        out_shape.append(jax.ShapeDtypeStruct((n, LANES), F32))
        out_specs.append(row(LANES))
    return pl.pallas_call(
        functools.partial(_merge_kernel, with_router=with_router),
        grid=(n // tm,),
        in_specs=in_specs, out_specs=out_specs, out_shape=out_shape,
        compiler_params=_cparams("parallel"),
        name="merge",
    )(*args)


def _ffn_kernel(x_ref, h_ref, wg_ref, wu_ref, wd_ref, gn_ref, o_ref, hn_ref):
    f = pl.program_id(1)
    h = h_ref[...]
    g = jnp.dot(h, wg_ref[0], preferred_element_type=F32)
    u = jnp.dot(h, wu_ref[0], preferred_element_type=F32)
    a = (g * _sigmoid(g) * u).astype(BF16)
    dlt = jnp.dot(a, wd_ref[0], preferred_element_type=F32)

    @pl.when(f == 0)
    def _():
        o_ref[...] = x_ref[...] + dlt

    @pl.when(f > 0)
    def _():
        o_ref[...] += dlt

    @pl.when(f == pl.num_programs(1) - 1)
    def _():
        hn_ref[...] = _rms_norm(o_ref[...], gn_ref[...]).astype(hn_ref.dtype)


def _dense_ffn(xf, h, w_gate, w_up, w_down, layer, next_gain):
    n, d = xf.shape
    ff = w_gate.shape[2]
    tm, tf = 512, 1408
    assert ff % tf == 0
    return pl.pallas_call(
        _ffn_kernel,
        grid=(n // tm, ff // tf),
        in_specs=[pl.BlockSpec((tm, d), lambda i, f: (i, 0)),
                  pl.BlockSpec((tm, d), lambda i, f: (i, 0)),
                  pl.BlockSpec((1, d, tf), lambda i, f: (layer, 0, f)),
                  pl.BlockSpec((1, d, tf), lambda i, f: (layer, 0, f)),
                  pl.BlockSpec((1, tf, d), lambda i, f: (layer, f, 0)),
                  pl.BlockSpec((1, d), lambda i, f: (0, 0))],
        out_specs=[pl.BlockSpec((tm, d), lambda i, f: (i, 0)), pl.BlockSpec((tm, d), lambda i, f: (i, 0))],
        out_shape=[jax.ShapeDtypeStruct((n, d), F32), jax.ShapeDtypeStruct((n, d), BF16)],
        compiler_params=_cparams("parallel", "arbitrary"),
        name="dense_ffn",
    )(xf, h, w_gate, w_up, w_down, next_gain.reshape(1, d))


def _row_copies(n_rows, make_copy):
    def start(i, carry):
        make_copy(i).start()
        return carry

    def wait(i, carry):
        make_copy(0).wait()
        return carry

    lax.fori_loop(0, n_rows, start, 0, unroll=8)
    lax.fori_loop(0, n_rows, wait, 0, unroll=8)


def _dispatch_kernel(dest_ref, hp_ref, xz_ref, xbuf_ref, sem, *, n_tok):
    del xz_ref

    def copy(a):
        tok = lax.shift_right_logical(a, 1)
        return pltpu.make_async_copy(hp_ref.at[pl.ds(tok, 1), :], xbuf_ref.at[pl.ds(dest_ref[0, 0, a], 1), :], sem)

    _row_copies(TOP_K * n_tok, copy)


def _dispatch(hp, dest3, p_rows):
    n, w = hp.shape
    steps, _, per_step = dest3.shape
    n_tok = per_step // TOP_K
    return pl.pallas_call(
        functools.partial(_dispatch_kernel, n_tok=n_tok),
        grid=(steps,),
        in_specs=[pl.BlockSpec((1, 1, per_step), lambda i: (i, 0, 0), memory_space=pltpu.SMEM),
                  pl.BlockSpec((n_tok, w), lambda i: (i, 0)),
                  pl.BlockSpec(memory_space=pl.ANY)],
        out_specs=pl.BlockSpec(memory_space=pl.ANY),
        out_shape=jax.ShapeDtypeStruct((p_rows, w), hp.dtype),
        scratch_shapes=[pltpu.SemaphoreType.DMA(())],
        input_output_aliases={2: 0},
        compiler_params=_cparams("arbitrary"),
        name="moe_dispatch",
    )(dest3, hp, jnp.zeros((p_rows, w), hp.dtype))


def _moe_kernel(te_ref, act_ref, x_ref, wg_ref, wu_ref, wd_ref, o_ref, xs_ref):
    i = pl.program_id(0)
    f = pl.program_id(1)
    active = act_ref[i] > 0
    half = xs_ref.shape[1] // 2

    @pl.when(active & (f == 0))
    def _():
        left, right = _unpack_bf16_pairs(x_ref[...])
        xs_ref[:, :half] = left
        xs_ref[:, half:] = right

    @pl.when(active)
    def _():
        x = xs_ref[...]
        g = jnp.dot(x, wg_ref[0, 0], preferred_element_type=F32)
        u = jnp.dot(x, wu_ref[0, 0], preferred_element_type=F32)
        a = (g * _sigmoid(g) * u).astype(BF16)
        dlt = jnp.dot(a, wd_ref[0, 0], preferred_element_type=F32)

        @pl.when(f == 0)
        def _():
            o_ref[...] = dlt

        @pl.when(f > 0)
        def _():
            o_ref[...] += dlt

    @pl.when(jnp.logical_not(active) & (f == 0))
    def _():
        o_ref[...] = jnp.zeros(o_ref.shape, o_ref.dtype)


def _moe_experts(xbuf, tile_expert, tile_active, w_gate, w_up, w_down, layer, tm):
    p, half = xbuf.shape
    d = 2 * half
    ff = w_gate.shape[3]
    tf = 1792
    assert ff % tf == 0
    nf = ff // tf

    def fsel(f, act, i):
        return jnp.where(act[i] > 0, f, nf - 1)

    grid_spec = pltpu.PrefetchScalarGridSpec(
        num_scalar_prefetch=2,
        grid=(p // tm, nf),
        in_specs=[pl.BlockSpec((tm, half), lambda i, f, te, act: (i, 0)),
                  pl.BlockSpec((1, 1, d, tf), lambda i, f, te, act: (layer, te[i], 0, fsel(f, act, i))),
                  pl.BlockSpec((1, 1, d, tf), lambda i, f, te, act: (layer, te[i], 0, fsel(f, act, i))),
                  pl.BlockSpec((1, 1, tf, d), lambda i, f, te, act: (layer, te[i], fsel(f, act, i), 0))],
        out_specs=pl.BlockSpec((tm, d), lambda i, f, te, act: (i, 0)),
        scratch_shapes=[pltpu.VMEM((tm, d), BF16)],
    )
    return pl.pallas_call(
        _moe_kernel,
        grid_spec=grid_spec,
        out_shape=jax.ShapeDtypeStruct((p, d), F32),
        compiler_params=_cparams("arbitrary", "arbitrary"),
        name="moe_experts",
    )(tile_expert, tile_active, xbuf, w_gate, w_up, w_down)


def _combine_kernel(dest_ref, x_ref, g_ref, gn_ref, ybuf_ref, o_ref, hn_ref, ys_ref, sem, *, n_tok):
    def copy(a):
        slot = (a & 1) * n_tok + lax.shift_right_logical(a, 1)
        return pltpu.make_async_copy(ybuf_ref.at[pl.ds(dest_ref[0, 0, a], 1), :], ys_ref.at[pl.ds(slot, 1), :], sem)

    _row_copies(TOP_K * n_tok, copy)
    g = g_ref[...]
    y = g[:, 0:1] * ys_ref[0:n_tok, :] + g[:, 1:2] * ys_ref[n_tok:, :]
    out = x_ref[...] + y
    o_ref[...] = out
    hn_ref[...] = _rms_norm(out, gn_ref[...]).astype(hn_ref.dtype)


def _combine(xf, gates, dest3, ybuf, next_gain):
    n, d = xf.shape
    steps, _, per_step = dest3.shape
    n_tok = per_step // TOP_K
    assert TOP_K == 2
    row = pl.BlockSpec((n_tok, d), lambda i: (i, 0))
    return pl.pallas_call(
        functools.partial(_combine_kernel, n_tok=n_tok),
        grid=(steps,),
        in_specs=[pl.BlockSpec((1, 1, per_step), lambda i: (i, 0, 0), memory_space=pltpu.SMEM),
                  row,
                  pl.BlockSpec((n_tok, TOP_K), lambda i: (i, 0)),
                  pl.BlockSpec((1, d), lambda i: (0, 0)),
                  pl.BlockSpec(memory_space=pl.ANY)],
        out_specs=[row, row],
        out_shape=[jax.ShapeDtypeStruct((n, d), F32), jax.ShapeDtypeStruct((n, d), BF16)],
        scratch_shapes=[pltpu.VMEM((TOP_K * n_tok, d), F32), pltpu.SemaphoreType.DMA(())],
        compiler_params=_cparams("arbitrary"),
        name="moe_combine",
    )(dest3, xf, gates, next_gain.reshape(1, d), ybuf)


def _moe(xf, hp, logits, w_gate, w_up, w_down, layer, next_gain):
    n, d = xf.shape
    na = n * TOP_K
    tm = 512
    n_tok = 512
    n_tiles = na // tm + N_EXPERTS
    top_vals, top_idx = lax.top_k(logits, TOP_K)
    gates = jax.nn.softmax(top_vals, axis=-1)
    e_flat = top_idx.reshape(-1)
    onehot = (e_flat[:, None] == jnp.arange(N_EXPERTS, dtype=e_flat.dtype)[None, :]).astype(jnp.int32)
    csum = jnp.cumsum(onehot, axis=0)
    pos_in_e = jnp.sum(csum * onehot, axis=1) - 1
    counts = csum[-1]
    tiles_per_e = (counts + tm - 1) // tm
    tile_end = jnp.cumsum(tiles_per_e)
    tile_start = tile_end - tiles_per_e
    dest = jnp.sum(tile_start[None, :] * onehot, axis=1) * tm + pos_in_e
    tile_ids = jnp.arange(n_tiles, dtype=jnp.int32)
    tile_expert = jnp.minimum(jnp.sum((tile_end[None, :] <= tile_ids[:, None]).astype(jnp.int32), axis=1),
                              N_EXPERTS - 1).astype(jnp.int32)
    tile_active = (tile_ids < tile_end[-1]).astype(jnp.int32)
    dest3 = dest.astype(jnp.int32).reshape(n // n_tok, 1, TOP_K * n_tok)
    xbuf = _dispatch(hp, dest3, n_tiles * tm)
    ybuf = _moe_experts(xbuf, tile_expert, tile_active, w_gate, w_up, w_down, layer, tm)
    return _combine(xf, gates, dest3, ybuf, next_gain)


def _rope(seq, dim):
    inv = 1.0 / (ROPE_THETA ** (jnp.arange(0, dim, 2, dtype=F32) / dim))
    ang = jnp.arange(seq, dtype=F32)[:, None] * inv[None, :]
    return jnp.cos(ang), jnp.sin(ang)


def _tables(seq):
    cos_r, sin_r = _rope(seq, RET_HEAD_DIM)
    cos_m, sin_m = _rope(seq, MOBA_HEAD_DIM)
    c_len = RET_CHUNK
    log_g = jnp.log(1.0 - jnp.exp2(-5.0 - jnp.arange(RET_HEADS, dtype=F32)))
    pos = jnp.arange(c_len, dtype=F32)
    diff = pos[:, None] - pos[None, :]
    decay = jnp.where(diff[None] >= 0, jnp.exp(diff[None] * log_g[:, None, None]), 0.0)
    w_state = jnp.exp((c_len - 1.0 - pos)[None, :] * log_g[:, None])
    w_cross = jnp.exp((pos + 1.0)[None, :] * log_g[:, None])
    g_chunk = jnp.exp(c_len * log_g)
    bcast = lambda a: jnp.broadcast_to(a[:, :, None], (RET_HEADS, c_len, RET_HEAD_DIM))
    return {
        "ret_cos": jnp.concatenate([cos_r, cos_r], axis=-1),
        "ret_sin": jnp.concatenate([-sin_r, sin_r], axis=-1),
        "ret_decay": decay,
        "ret_wstate": bcast(w_state),
        "ret_wcross": bcast(w_cross),
        "ret_gchunk": jnp.broadcast_to(g_chunk[:, None, None], (RET_HEADS, RET_HEAD_DIM, RET_HEAD_DIM)),
        "moba_cos": jnp.tile(cos_m, (1, 2 * LANES // MOBA_HEAD_DIM)),
        "moba_sin": jnp.tile(jnp.concatenate([-sin_m, sin_m], axis=-1), (1, LANES // MOBA_HEAD_DIM)),
    }


def kernel(x, g_mix, w_in, conv_w, conv_b, conv_ln_g, conv_ln_b, q_norm_g, k_norm_g, w_ret_o, w_conv_o,
           w_moba_o, w_out, g_ffn, w_ff_gate, w_ff_up, w_ff_down, w_router, w_e_gate, w_e_up, w_e_down):
    batch, seq, d = x.shape
    depth = g_mix.shape[0]
    assert d == D_MODEL and w_in.shape[2] == IN_COLS
    xf = x.reshape(batch * seq, d)
    tabs = _tables(seq)
    w_ret_o, w_conv_o, w_moba_o, w_out, w_ff_gate, w_ff_up, w_ff_down, w_e_gate, w_e_up, w_e_down = (
        w.astype(BF16) for w in (w_ret_o, w_conv_o, w_moba_o, w_out, w_ff_gate, w_ff_up, w_ff_down,
                                 w_e_gate, w_e_up, w_e_down))
    hn = _rmsnorm(xf, g_mix[0])
    for l in range(depth):
        next_gain = g_mix[l + 1] if l + 1 < depth else jnp.ones((d,), F32)
        proj = _inproj(hn, w_in, l)
        yr = _retention(proj, tabs, batch, seq)
        yc = _conv_branch(proj, conv_w[l], conv_b[l], conv_ln_g[l], conv_ln_b[l], batch, seq)
        ym = _moba(proj, q_norm_g[l], k_norm_g[l], tabs, batch, seq)
        j = l // 2
        if l % 2 == 0:
            xf, h = _merge(xf, proj, yr, yc, ym, w_ret_o, w_conv_o, w_moba_o, w_out, g_ffn[l], None, l)
            xf, hn = _dense_ffn(xf, h, w_ff_gate, w_ff_up, w_ff_down, j, next_gain)
        else:
            xf, hp, logits = _merge(xf, proj, yr, yc, ym, w_ret_o, w_conv_o, w_moba_o, w_out, g_ffn[l],
                                    w_router[j], l)
            xf, hn = _moe(xf, hp, logits[:, :N_EXPERTS], w_e_gate, w_e_up, w_e_down, j, next_gain)
    return xf.reshape(batch, seq, d)
```

```python
import functools

import jax
import jax.numpy as jnp
from jax import lax
from jax.experimental import pallas as pl
from jax.experimental.pallas import tpu as pltpu

F32 = jnp.float32
BF16 = jnp.bfloat16

D_MODEL = 1024
RET_HEADS = 4
RET_HEAD_DIM = 128
RET_W = RET_HEADS * RET_HEAD_DIM
RET_CHUNK = 128
CONV_CH = 512
CONV_WIDTH = 31
MOBA_HEADS = 8
MOBA_HEAD_DIM = 64
MOBA_W = MOBA_HEADS * MOBA_HEAD_DIM
MOBA_BLOCK = 256
MOBA_TOPK = 3
ROPE_THETA = 10000.0
N_EXPERTS = 8
TOP_K = 2
NORM_EPS = 1e-6
GROUP_NORM_EPS = 1e-5

OFF_RQ, OFF_RK, OFF_RV, OFF_RG = 0, 512, 1024, 1536
OFF_CA, OFF_CB = 2048, 2560
OFF_MQ, OFF_MK, OFF_MV = 3072, 3584, 4096
OFF_GATES = 4608
IN_COLS = OFF_GATES + 3 * D_MODEL

LANES = 128
SUBLANES = 8
CONV_HALO = 32
VMEM_LIMIT_BYTES = 56 * 1024 * 1024
NEG_BIG = -1e30
LOG2_E = 1.4426950408889634

_NT = (((1,), (1,)), ((), ()))
_TN = (((0,), (0,)), ((), ()))


def _cparams(*sem):
    return pltpu.CompilerParams(dimension_semantics=sem, vmem_limit_bytes=VMEM_LIMIT_BYTES)


def _sigmoid(x):
    return 1.0 / (1.0 + jnp.exp(-x))


def _pack_bf16_pairs(x):
    w = x.shape[1] // 2
    bits = pltpu.bitcast(x.astype(BF16).astype(F32), jnp.uint32)
    return lax.shift_right_logical(bits[:, :w], jnp.uint32(16)) | bits[:, w:]


def _unpack_bf16_pairs(words):
    left = pltpu.bitcast(lax.shift_left(words, jnp.uint32(16)), F32)
    right = pltpu.bitcast(words & jnp.uint32(0xFFFF0000), F32)
    return left.astype(BF16), right.astype(BF16)


def _rms_norm(x, gain):
    ms = jnp.mean(x * x, axis=-1, keepdims=True)
    return x * lax.rsqrt(ms + NORM_EPS) * gain


def _rmsnorm_kernel(x_ref, g_ref, o_ref):
    o_ref[...] = _rms_norm(x_ref[...], g_ref[...]).astype(o_ref.dtype)


def _rmsnorm(xf, gain):
    n, d = xf.shape
    tm = 1024
    return pl.pallas_call(
        _rmsnorm_kernel,
        grid=(n // tm,),
        in_specs=[pl.BlockSpec((tm, d), lambda i: (i, 0)), pl.BlockSpec((1, d), lambda i: (0, 0))],
        out_specs=pl.BlockSpec((tm, d), lambda i: (i, 0)),
        out_shape=jax.ShapeDtypeStruct((n, d), BF16),
        compiler_params=_cparams("parallel"),
        name="rmsnorm",
    )(xf, gain.reshape(1, d))


def _inproj_kernel(h_ref, w_ref, o_ref, wb_ref):
    @pl.when(pl.program_id(1) == 0)
    def _():
        wb_ref[...] = w_ref[0].astype(BF16)

    o_ref[...] = jnp.dot(h_ref[...], wb_ref[...], preferred_element_type=F32).astype(o_ref.dtype)


def _inproj(h, w_in, layer):
    n, d = h.shape
    c = w_in.shape[2]
    tm, tn = 1024, 1536
    return pl.pallas_call(
        _inproj_kernel,
        grid=(c // tn, n // tm),
        in_specs=[pl.BlockSpec((tm, d), lambda j, i: (i, 0)),
                  pl.BlockSpec((1, d, tn), lambda j, i: (layer, 0, j))],
        out_specs=pl.BlockSpec((tm, tn), lambda j, i: (i, j)),
        out_shape=jax.ShapeDtypeStruct((n, c), BF16),
        scratch_shapes=[pltpu.VMEM((d, tn), BF16)],
        compiler_params=_cparams("arbitrary", "arbitrary"),
        name="inproj",
    )(h, w_in)


def _ret_kernel(q_ref, k_ref, v_ref, g_ref, cos_ref, sin_ref, dec_ref, wst_ref, wcr_ref, gch_ref,
                o_ref, state_ref, *, n_chunks):
    c_len = RET_CHUNK
    hd = RET_HEAD_DIM
    state_ref[...] = jnp.zeros(state_ref.shape, F32)

    def body(c, carry):
        r = pl.ds(pl.multiple_of(c * c_len, c_len), c_len)
        cs = cos_ref[r, :]
        sn = sin_ref[r, :]
        for h in range(RET_HEADS):
            hs = slice(h * hd, (h + 1) * hd)
            q = q_ref[r, hs].astype(F32)
            k = k_ref[r, hs].astype(F32)
            q = q * cs + pltpu.roll(q, hd // 2, 1) * sn
            k = (k * cs + pltpu.roll(k, hd // 2, 1) * sn) * (hd ** -0.5)
            qb = q.astype(BF16)
            kb = k.astype(BF16)
            vb = v_ref[r, hs]
            state = state_ref[h]
            s = lax.dot_general(qb, kb, _NT, preferred_element_type=F32) * dec_ref[h]
            y = jnp.dot(s.astype(BF16), vb, preferred_element_type=F32)
            y = y + jnp.dot(qb, state.astype(BF16), preferred_element_type=F32) * wcr_ref[h]
            kw = (k * wst_ref[h]).astype(BF16)
            kv = lax.dot_general(kw, vb, _TN, preferred_element_type=F32)
            state_ref[h] = gch_ref[h] * state + kv
            mu = jnp.mean(y, axis=-1, keepdims=True)
            yc = y - mu
            var = jnp.mean(yc * yc, axis=-1, keepdims=True)
            yn = yc * lax.rsqrt(var + GROUP_NORM_EPS)
            g = g_ref[r, hs].astype(F32)
            o_ref[r, hs] = (g * _sigmoid(g) * yn).astype(o_ref.dtype)
        return carry

    lax.fori_loop(0, n_chunks, body, 0, unroll=2)


def _retention(proj, tabs, batch, seq):
    n = proj.shape[0]
    hd = RET_HEAD_DIM

    def col(off):
        return pl.BlockSpec((seq, RET_W), lambda b, off=off: (b, off // RET_W))

    tab = pl.BlockSpec((seq, hd), lambda b: (0, 0))
    htab = pl.BlockSpec((RET_HEADS, hd, hd), lambda b: (0, 0, 0))
    return pl.pallas_call(
        functools.partial(_ret_kernel, n_chunks=seq // RET_CHUNK),
        grid=(batch,),
        in_specs=[col(OFF_RQ), col(OFF_RK), col(OFF_RV), col(OFF_RG), tab, tab, htab, htab, htab, htab],
        out_specs=pl.BlockSpec((seq, RET_W), lambda b: (b, 0)),
        out_shape=jax.ShapeDtypeStruct((n, RET_W), BF16),
        scratch_shapes=[pltpu.VMEM((RET_HEADS, hd, hd), F32)],
        compiler_params=_cparams("parallel"),
        name="retention",
    )(proj, proj, proj, proj, tabs["ret_cos"], tabs["ret_sin"], tabs["ret_decay"], tabs["ret_wstate"],
      tabs["ret_wcross"], tabs["ret_gchunk"])


def _conv_kernel(ca_ref, cb_ref, w_ref, b_ref, lg_ref, lb_ref, o_ref, ext_ref, sh_ref, *, t_tile, r_chunk):
    t = pl.program_id(1)

    @pl.when(t == 0)
    def _():
        ext_ref[0:CONV_HALO, :] = jnp.zeros((CONV_HALO, CONV_CH), F32)

    @pl.when(t > 0)
    def _():
        ext_ref[0:CONV_HALO, :] = ext_ref[t_tile:t_tile + CONV_HALO, :]

    ca = ca_ref[...].astype(F32)
    cb = cb_ref[...].astype(F32)
    ext_ref[CONV_HALO:, :] = ca * _sigmoid(cb)

    first_tap = CONV_HALO - (CONV_WIDTH - 1)
    shifted_rows = sh_ref.shape[1]
    for s in range(1, SUBLANES):
        sh_ref[s] = ext_ref[s:s + shifted_rows, :]
    bias = b_ref[...]
    lg = lg_ref[...]
    lb = lb_ref[...]
    for r0 in range(0, t_tile, r_chunk):
        acc = jnp.zeros((r_chunk, CONV_CH), F32) + bias
        for j in range(CONV_WIDTH):
            off = first_tap + j
            s = off % SUBLANES
            rows = pl.ds(r0 + off - s, r_chunk)
            tap = ext_ref[rows, :] if s == 0 else sh_ref[s, rows, :]
            acc = acc + tap * w_ref[j:j + 1, :]
        mu = jnp.mean(acc, axis=-1, keepdims=True)
        yc = acc - mu
        var = jnp.mean(yc * yc, axis=-1, keepdims=True)
        y = yc * lax.rsqrt(var + NORM_EPS) * lg + lb
        o_ref[r0:r0 + r_chunk, :] = (y * _sigmoid(y)).astype(o_ref.dtype)


def _conv_branch(proj, conv_w, conv_b, ln_g, ln_b, batch, seq):
    n = proj.shape[0]
    t_tile = 512
    tiles = seq // t_tile
    w = jnp.zeros((CONV_HALO, CONV_CH), F32).at[:CONV_WIDTH].set(conv_w)
    vec = pl.BlockSpec((1, CONV_CH), lambda b, t: (0, 0))

    def col(off):
        return pl.BlockSpec((t_tile, CONV_CH), lambda b, t, off=off: (b * tiles + t, off // CONV_CH))

    return pl.pallas_call(
        functools.partial(_conv_kernel, t_tile=t_tile, r_chunk=64),
        grid=(batch, tiles),
        in_specs=[col(OFF_CA), col(OFF_CB), pl.BlockSpec((CONV_HALO, CONV_CH), lambda b, t: (0, 0)),
                  vec, vec, vec],
        out_specs=pl.BlockSpec((t_tile, CONV_CH), lambda b, t: (b * tiles + t, 0)),
        out_shape=jax.ShapeDtypeStruct((n, CONV_CH), BF16),
        scratch_shapes=[pltpu.VMEM((CONV_HALO + t_tile, CONV_CH), F32),
                        pltpu.VMEM((SUBLANES, CONV_HALO + t_tile - SUBLANES, CONV_CH), F32)],
        compiler_params=_cparams("parallel", "arbitrary"),
        name="conv_branch",
    )(proj, proj, w, conv_b.reshape(1, -1), ln_g.reshape(1, -1), ln_b.reshape(1, -1))


def _moba_kernel(q_ref, k_ref, v_ref, qg_ref, kg_ref, cos_ref, sin_ref, o_ref, *scratch, n_blocks):
    def pair(p, carry):
        pc = pl.ds(pl.multiple_of(p * LANES, LANES), LANES)
        _moba_pair(pc, q_ref, k_ref, v_ref, qg_ref, kg_ref, cos_ref, sin_ref, o_ref, *scratch, n_blocks=n_blocks)
        return carry

    lax.fori_loop(0, q_ref.shape[1] // LANES, pair, 0)


def _moba_pair(pc, q_ref, k_ref, v_ref, qg_ref, kg_ref, cos_ref, sin_ref, o_ref,
               qf_s, q0_s, q1_s, k0_s, k1_s, km_s, *, n_blocks):
    bs = MOBA_BLOCK
    hd = MOBA_HEAD_DIM
    lane = lax.broadcasted_iota(jnp.int32, (1, LANES), 1)
    head0 = lane < hd
    first_half = (lane % hd) < (hd // 2)
    q_scale = (hd ** -0.5) * LOG2_E
    qg = qg_ref[...]
    kg = kg_ref[...]
    same_head = ((lax.broadcasted_iota(jnp.int32, (LANES, LANES), 0) < hd)
                 == (lax.broadcasted_iota(jnp.int32, (LANES, LANES), 1) < hd))
    head_mean = jnp.where(same_head, 1.0 / hd, 0.0).astype(BF16)

    def norm_rope(x, g, cs, sn):
        x2 = x * x
        hi = x2.astype(BF16)
        lo = (x2 - hi.astype(F32)).astype(BF16)
        ms = (jnp.dot(hi, head_mean, preferred_element_type=F32)
              + jnp.dot(lo, head_mean, preferred_element_type=F32))
        y = x * lax.rsqrt(ms + NORM_EPS) * g
        partner = jnp.where(first_half, pltpu.roll(y, LANES - hd // 2, 1), pltpu.roll(y, hd // 2, 1))
        return y * cs + partner * sn

    km_s[...] = jnp.zeros(km_s.shape, F32)

    def prologue(j, carry):
        r = pl.ds(pl.multiple_of(j * bs, bs), bs)
        cs = cos_ref[r, :]
        sn = sin_ref[r, :]
        qn = norm_rope(q_ref[r, pc].astype(F32), qg, cs, sn)
        kn = norm_rope(k_ref[r, pc].astype(F32), kg, cs, sn)
        qf_s[r, :] = qn
        qsc = qn * q_scale
        q0_s[r, :] = jnp.where(head0, qsc, 0.0).astype(BF16)
        q1_s[r, :] = jnp.where(head0, 0.0, qsc).astype(BF16)
        k0_s[r, :] = jnp.where(head0, kn, jnp.where(lane == hd + j, 1.0, 0.0)).astype(BF16)
        k1_s[r, :] = jnp.where(head0, jnp.where(lane == j, 1.0, 0.0), kn).astype(BF16)
        kmean = jnp.sum(kn, axis=0, keepdims=True) * (1.0 / bs)
        km_s[pl.ds(hd + j, 1), :] = jnp.where(head0, kmean, 0.0)
        km_s[pl.ds(j, 1), :] = jnp.where(head0, 0.0, kmean)
        return carry

    lax.fori_loop(0, n_blocks, prologue, 0, unroll=2)

    causal =(lax.broadcasted_iota(jnp.int32, (bs, bs), 0) >= lax.broadcasted_iota(jnp.int32, (bs, bs), 1))
    sub = lax.broadcasted_iota(jnp.int32, (8, bs), 0)

    def bias_rows(grp, b):
        rank = jnp.zeros(grp.shape, F32)
        for jp in range(b):
            row = grp[jp:jp + 1, :]
            ahead = (row > grp) | ((row == grp) & (jp < sub))
            rank = rank + jnp.where(ahead, 1.0, 0.0)
        return jnp.where((rank < float(MOBA_TOPK)) | (sub >= b), 0.0, NEG_BIG)

    for b in range(n_blocks):
        rq = slice(b * bs, (b + 1) * bs)
        past = b * bs
        if b > MOBA_TOPK:
            gate_t = lax.dot_general(km_s[...], qf_s[rq, :], _NT, precision=lax.Precision.HIGHEST,
                                     preferred_element_type=F32)
            pad = jnp.zeros((hd - 8, bs), F32)
            bias_t = jnp.concatenate([bias_rows(gate_t[0:8, :], b), pad,
                                      bias_rows(gate_t[hd:hd + 8, :], b), pad], axis=0)
            bias = bias_t.T
            qsc = qf_s[rq, :] * q_scale
            q_aug = (jnp.where(head0, qsc, bias).astype(BF16), jnp.where(head0, bias, qsc).astype(BF16))
        else:
            q_aug = (q0_s[rq, :], q1_s[rq, :])
        outs = []
        for h in range(2):
            ks = (k0_s, k1_s)[h]
            s_d = lax.dot_general(q_aug[h], ks[rq, :], _NT, preferred_element_type=F32)
            s_d = jnp.where(causal, s_d, NEG_BIG)
            m = jnp.max(s_d, axis=-1, keepdims=True)
            if past:
                s_p = lax.dot_general(q_aug[h], ks[0:past, :], _NT, preferred_element_type=F32)
                m = jnp.maximum(m, jnp.max(s_p, axis=-1, keepdims=True))
            p_d = jnp.exp2(s_d - m)
            l = jnp.sum(p_d, axis=-1, keepdims=True)
            acc = jnp.dot(p_d.astype(BF16), v_ref[rq, pc], preferred_element_type=F32)
            if past:
                p_p = jnp.exp2(s_p - m)
                l = l + jnp.sum(p_p, axis=-1, keepdims=True)
                acc = acc + jnp.dot(p_p.astype(BF16), v_ref[0:past, pc], preferred_element_type=F32)
            outs.append(acc / l)
        o_ref[rq, pc] = jnp.where(head0, outs[0], outs[1]).astype(o_ref.dtype)


def _moba(proj, q_gain, k_gain, tabs, batch, seq):
    n = proj.shape[0]
    assert seq % MOBA_BLOCK == 0
    n_blocks = seq // MOBA_BLOCK
    assert n_blocks <= 8

    def col(off):
        return pl.BlockSpec((seq, MOBA_W), lambda b, off=off: (b, off // MOBA_W))

    tab = pl.BlockSpec((seq, LANES), lambda b: (0, 0))
    vec = pl.BlockSpec((1, LANES), lambda b: (0, 0))
    return pl.pallas_call(
        functools.partial(_moba_kernel, n_blocks=n_blocks),
        grid=(batch,),
        in_specs=[col(OFF_MQ), col(OFF_MK), col(OFF_MV), vec, vec, tab, tab],
        out_specs=pl.BlockSpec((seq, MOBA_W), lambda b: (b, 0)),
        out_shape=jax.ShapeDtypeStruct((n, MOBA_W), BF16),
        scratch_shapes=[pltpu.VMEM((seq, LANES), F32), pltpu.VMEM((seq, LANES), BF16),
                        pltpu.VMEM((seq, LANES), BF16), pltpu.VMEM((seq, LANES), BF16),
                        pltpu.VMEM((seq, LANES), BF16), pltpu.VMEM((LANES, LANES), F32)],
        compiler_params=_cparams("parallel"),
        name="moba",
    )(proj, proj, proj, jnp.tile(q_gain, 2).reshape(1, LANES), jnp.tile(k_gain, 2).reshape(1, LANES),
      tabs["moba_cos"], tabs["moba_sin"])


def _merge_kernel(x_ref, ga_ref, gb_ref, yr_ref, yc_ref, ym_ref, wr_ref, wc_ref, wm_ref, wo_ref, gf_ref,
                  *rest, with_router):
    if with_router:
        whi_ref, wlo_ref, xo_ref, h_ref, lg_ref = rest
    else:
        xo_ref, h_ref = rest
    d = D_MODEL
    ga = ga_ref[...].astype(F32)
    gb = gb_ref[...].astype(F32)
    g_ret = ga[:, :d]
    g_conv = jnp.concatenate([ga[:, d:], gb[:, :d // 2]], axis=-1)
    g_moba = gb[:, d // 2:]
    merged = (_sigmoid(g_ret) * jnp.dot(yr_ref[...], wr_ref[0], preferred_element_type=F32)
              + _sigmoid(g_conv) * jnp.dot(yc_ref[...], wc_ref[0], preferred_element_type=F32)
              + _sigmoid(g_moba) * jnp.dot(ym_ref[...], wm_ref[0], preferred_element_type=F32))
    x = x_ref[...] + jnp.dot(merged.astype(BF16), wo_ref[0], preferred_element_type=F32)
    xo_ref[...] = x
    h = _rms_norm(x, gf_ref[...])
    if with_router:
        h_ref[...] = _pack_bf16_pairs(h)
        h_hi = h.astype(BF16)
        h_lo = (h - h_hi.astype(F32)).astype(BF16)
        lg_ref[...] = (jnp.dot(h_hi, whi_ref[...], preferred_element_type=F32)
                       + jnp.dot(h_hi, wlo_ref[...], preferred_element_type=F32)
                       + jnp.dot(h_lo, whi_ref[...], preferred_element_type=F32))
    else:
        h_ref[...] = h.astype(BF16)


def _merge(xf, proj, yr, yc, ym, w_ret_o, w_conv_o, w_moba_o, w_out, g_ffn, w_router, layer):
    n, d = xf.shape
    tm = 512
    gw = 3 * d // 2
    with_router = w_router is not None
    row = lambda w: pl.BlockSpec((tm, w), lambda i: (i, 0))
    full = lambda a: pl.BlockSpec(a.shape, lambda i: (0,) * a.ndim)
    of_layer = lambda a: pl.BlockSpec((1,) + a.shape[1:], lambda i: (layer,) + (0,) * (a.ndim - 1))
    gf = g_ffn.reshape(1, d)
    args = [xf, proj, proj, yr, yc, ym, w_ret_o, w_conv_o, w_moba_o, w_out, gf]
    in_specs = [row(d),
                pl.BlockSpec((tm, gw), lambda i: (i, OFF_GATES // gw)),
                pl.BlockSpec((tm, gw), lambda i: (i, OFF_GATES // gw + 1)),
                row(RET_W), row(CONV_CH), row(MOBA_W), of_layer(w_ret_o), of_layer(w_conv_o),
                of_layer(w_moba_o), of_layer(w_out), full(gf)]
    if with_router:
        out_shape = [jax.ShapeDtypeStruct((n, d), F32), jax.ShapeDtypeStruct((n, d // 2), jnp.uint32)]
        out_specs = [row(d), row(d // 2)]
    else:
        out_shape = [jax.ShapeDtypeStruct((n, d), F32), jax.ShapeDtypeStruct((n, d), BF16)]
        out_specs = [row(d), row(d)]
    if with_router:
        wrt = jnp.zeros((d, LANES), F32).at[:, :N_EXPERTS].set(w_router)
        w_hi = wrt.astype(BF16)
        w_lo = (wrt - w_hi.astype(F32)).astype(BF16)
        args += [w_hi, w_lo]
        in_specs += [full(w_hi), full(w_lo)]
        out_shape.append(jax.ShapeDtypeStruct((n, LANES), F32))
        out_specs.append(row(LANES))
    return pl.pallas_call(
        functools.partial(_merge_kernel, with_router=with_router),
        grid=(n // tm,),
        in_specs=in_specs, out_specs=out_specs, out_shape=out_shape,
        compiler_params=_cparams("parallel"),
        name="merge",
    )(*args)


def _ffn_kernel(x_ref, h_ref, wg_ref, wu_ref, wd_ref, gn_ref, o_ref, hn_ref):
    f = pl.program_id(1)
    h = h_ref[...]
    g = jnp.dot(h, wg_ref[0], preferred_element_type=F32)
    u = jnp.dot(h, wu_ref[0], preferred_element_type=F32)
    a = (g * _sigmoid(g) * u).astype(BF16)
    dlt = jnp.dot(a, wd_ref[0], preferred_element_type=F32)

    @pl.when(f == 0)
    def _():
        o_ref[...] = x_ref[...] + dlt

    @pl.when(f > 0)
    def _():
        o_ref[...] += dlt

    @pl.when(f == pl.num_programs(1) - 1)
    def _():
        hn_ref[...] = _rms_norm(o_ref[...], gn_ref[...]).astype(hn_ref.dtype)


def _dense_ffn(xf, h, w_gate, w_up, w_down, layer, next_gain):
    n, d = xf.shape
    ff = w_gate.shape[2]
    tm, tf = 512, 1408
    assert ff % tf == 0
    return pl.pallas_call(
        _ffn_kernel,
        grid=(n // tm, ff // tf),
        in_specs=[pl.BlockSpec((tm, d), lambda i, f: (i, 0)),
                  pl.BlockSpec((tm, d), lambda i, f: (i, 0)),
                  pl.BlockSpec((1, d, tf), lambda i, f: (layer, 0, f)),
                  pl.BlockSpec((1, d, tf), lambda i, f: (layer, 0, f)),
                  pl.BlockSpec((1, tf, d), lambda i, f: (layer, f, 0)),
                  pl.BlockSpec((1, d), lambda i, f: (0, 0))],
        out_specs=[pl.BlockSpec((tm, d), lambda i, f: (i, 0)), pl.BlockSpec((tm, d), lambda i, f: (i, 0))],
        out_shape=[jax.ShapeDtypeStruct((n, d), F32), jax.ShapeDtypeStruct((n, d), BF16)],
        compiler_params=_cparams("parallel", "arbitrary"),
        name="dense_ffn",
    )(xf, h, w_gate, w_up, w_down, next_gain.reshape(1, d))


def _row_copies(n_groups, group_copies):
    def start(g, carry):
        for cp in group_copies(g):
            cp.start()
        return carry

    def wait(g, carry):
        for cp in group_copies(0):
            cp.wait()
        return carry

    lax.fori_loop(0, n_groups, start, 0, unroll=2)
    lax.fori_loop(0, n_groups, wait, 0, unroll=2)


def _dispatch_kernel(dest_ref, hp_ref, xz_ref, xbuf_ref, sem, *, n_tok):
    del xz_ref

    def group(g):
        tok0 = pl.multiple_of(g * SUBLANES, SUBLANES)
        return [pltpu.make_async_copy(hp_ref.at[pl.ds(tok0 + r, 1), :],
                                      xbuf_ref.at[pl.ds(dest_ref[0, 0, TOP_K * (tok0 + r) + k], 1), :], sem)
                for r in range(SUBLANES) for k in range(TOP_K)]

    _row_copies(n_tok // SUBLANES, group)


def _dispatch(hp, dest3, p_rows):
    n, w = hp.shape
    steps, _, per_step = dest3.shape
    n_tok = per_step // TOP_K
    return pl.pallas_call(
        functools.partial(_dispatch_kernel, n_tok=n_tok),
        grid=(steps,),
        in_specs=[pl.BlockSpec((1, 1, per_step), lambda i: (i, 0, 0), memory_space=pltpu.SMEM),
                  pl.BlockSpec((n_tok, w), lambda i: (i, 0)),
                  pl.BlockSpec(memory_space=pl.ANY)],
        out_specs=pl.BlockSpec(memory_space=pl.ANY),
        out_shape=jax.ShapeDtypeStruct((p_rows, w), hp.dtype),
        scratch_shapes=[pltpu.SemaphoreType.DMA(())],
        input_output_aliases={2: 0},
        compiler_params=_cparams("arbitrary"),
        name="moe_dispatch",
    )(dest3, hp, jnp.zeros((p_rows, w), hp.dtype))


def _moe_kernel(te_ref, act_ref, x_ref, wg_ref, wu_ref, wd_ref, o_ref, xs_ref):
    i = pl.program_id(0)
    f = pl.program_id(1)
    active = act_ref[i] > 0
    half = xs_ref.shape[1] // 2

    @pl.when(active & (f == 0))
    def _():
        left, right = _unpack_bf16_pairs(x_ref[...])
        xs_ref[:, :half] = left
        xs_ref[:, half:] = right

    @pl.when(active)
    def _():
        x = xs_ref[...]
        g = jnp.dot(x, wg_ref[0, 0], preferred_element_type=F32)
        u = jnp.dot(x, wu_ref[0, 0], preferred_element_type=F32)
        a = (g * _sigmoid(g) * u).astype(BF16)
        dlt = jnp.dot(a, wd_ref[0, 0], preferred_element_type=F32)

        @pl.when(f == 0)
        def _():
            o_ref[...] = dlt

        @pl.when(f > 0)
        def _():
            o_ref[...] += dlt

    @pl.when(jnp.logical_not(active) & (f == 0))
    def _():
        o_ref[...] = jnp.zeros(o_ref.shape, o_ref.dtype)


def _moe_experts(xbuf, tile_expert, tile_active, w_gate, w_up, w_down, layer, tm):
    p, half = xbuf.shape
    d = 2 * half
    ff = w_gate.shape[3]
    tf = 1792
    assert ff % tf == 0
    nf = ff // tf

    def fsel(f, act, i):
        return jnp.where(act[i] > 0, f, nf - 1)

    grid_spec = pltpu.PrefetchScalarGridSpec(
        num_scalar_prefetch=2,
        grid=(p // tm, nf),
        in_specs=[pl.BlockSpec((tm, half), lambda i, f, te, act: (i, 0)),
                  pl.BlockSpec((1, 1, d, tf), lambda i, f, te, act: (layer, te[i], 0, fsel(f, act, i))),
                  pl.BlockSpec((1, 1, d, tf), lambda i, f, te, act: (layer, te[i], 0, fsel(f, act, i))),
                  pl.BlockSpec((1, 1, tf, d), lambda i, f, te, act: (layer, te[i], fsel(f, act, i), 0))],
        out_specs=pl.BlockSpec((tm, d), lambda i, f, te, act: (i, 0)),
        scratch_shapes=[pltpu.VMEM((tm, d), BF16)],
    )
    return pl.pallas_call(
        _moe_kernel,
        grid_spec=grid_spec,
        out_shape=jax.ShapeDtypeStruct((p, d), F32),
        compiler_params=_cparams("arbitrary", "arbitrary"),
        name="moe_experts",
    )(tile_expert, tile_active, xbuf, w_gate, w_up, w_down)


def _combine_kernel(dest_ref, x_ref, g_ref, gn_ref, ybuf_ref, o_ref, hn_ref, ys_ref, sem, *, n_tok):
    def group(g):
        tok0 = pl.multiple_of(g * SUBLANES, SUBLANES)
        return [pltpu.make_async_copy(ybuf_ref.at[pl.ds(dest_ref[0, 0, TOP_K * (tok0 + r) + k], 1), :],
                                      ys_ref.at[pl.ds(k * n_tok + tok0 + r, 1), :], sem)
                for r in range(SUBLANES) for k in range(TOP_K)]

    _row_copies(n_tok // SUBLANES, group)
    g = g_ref[...]
    y = g[:, 0:1] * ys_ref[0:n_tok, :] + g[:, 1:2] * ys_ref[n_tok:, :]
    out = x_ref[...] + y
    o_ref[...] = out
    hn_ref[...] = _rms_norm(out, gn_ref[...]).astype(hn_ref.dtype)


def _combine(xf, gates, dest3, ybuf, next_gain):
    n, d = xf.shape
    steps, _, per_step = dest3.shape
    n_tok = per_step // TOP_K
    assert TOP_K == 2
    row = pl.BlockSpec((n_tok, d), lambda i: (i, 0))
    return pl.pallas_call(
        functools.partial(_combine_kernel, n_tok=n_tok),
        grid=(steps,),
        in_specs=[pl.BlockSpec((1, 1, per_step), lambda i: (i, 0, 0), memory_space=pltpu.SMEM),
                  row,
                  pl.BlockSpec((n_tok, TOP_K), lambda i: (i, 0)),
                  pl.BlockSpec((1, d), lambda i: (0, 0)),
                  pl.BlockSpec(memory_space=pl.ANY)],
        out_specs=[row, row],
        out_shape=[jax.ShapeDtypeStruct((n, d), F32), jax.ShapeDtypeStruct((n, d), BF16)],
        scratch_shapes=[pltpu.VMEM((TOP_K * n_tok, d), F32), pltpu.SemaphoreType.DMA(())],
        compiler_params=_cparams("arbitrary"),
        name="moe_combine",
    )(dest3, xf, gates, next_gain.reshape(1, d), ybuf)


def _moe(xf, hp, logits, w_gate, w_up, w_down, layer, next_gain):
    n, d = xf.shape
    na = n * TOP_K
    tm = 512
    n_tok = 512
    n_tiles = na // tm + N_EXPERTS
    top_vals, top_idx = lax.top_k(logits, TOP_K)
    gates = jax.nn.softmax(top_vals, axis=-1)
    e_flat = top_idx.reshape(-1)
    onehot = (e_flat[:, None] == jnp.arange(N_EXPERTS, dtype=e_flat.dtype)[None, :]).astype(jnp.int32)
    csum = jnp.cumsum(onehot, axis=0)
    pos_in_e = jnp.sum(csum * onehot, axis=1) - 1
    counts = csum[-1]
    tiles_per_e = (counts + tm - 1) // tm
    tile_end = jnp.cumsum(tiles_per_e)
    tile_start = tile_end - tiles_per_e
    dest = jnp.sum(tile_start[None, :] * onehot, axis=1) * tm + pos_in_e
    tile_ids = jnp.arange(n_tiles, dtype=jnp.int32)
    tile_expert = jnp.minimum(jnp.sum((tile_end[None, :] <= tile_ids[:, None]).astype(jnp.int32), axis=1),
                              N_EXPERTS - 1).astype(jnp.int32)
    tile_active = (tile_ids < tile_end[-1]).astype(jnp.int32)
    dest3 = dest.astype(jnp.int32).reshape(n // n_tok, 1, TOP_K * n_tok)
    xbuf = _dispatch(hp, dest3, n_tiles * tm)
    ybuf = _moe_experts(xbuf, tile_expert, tile_active, w_gate, w_up, w_down, layer, tm)
    return _combine(xf, gates, dest3, ybuf, next_gain)


def _rope(seq, dim):
    inv = 1.0 / (ROPE_THETA ** (jnp.arange(0, dim, 2, dtype=F32) / dim))
    ang = jnp.arange(seq, dtype=F32)[:, None] * inv[None, :]
    return jnp.cos(ang), jnp.sin(ang)


def _tables(seq):
    cos_r, sin_r = _rope(seq, RET_HEAD_DIM)
    cos_m, sin_m = _rope(seq, MOBA_HEAD_DIM)
    c_len = RET_CHUNK
    log_g = jnp.log(1.0 - jnp.exp2(-5.0 - jnp.arange(RET_HEADS, dtype=F32)))
    pos = jnp.arange(c_len, dtype=F32)
    diff = pos[:, None] - pos[None, :]
    decay = jnp.where(diff[None] >= 0, jnp.exp(diff[None] * log_g[:, None, None]), 0.0)
    w_state = jnp.exp((c_len - 1.0 - pos)[None, :] * log_g[:, None])
    w_cross = jnp.exp((pos + 1.0)[None, :] * log_g[:, None])
    g_chunk = jnp.exp(c_len * log_g)
    bcast = lambda a: jnp.broadcast_to(a[:, :, None], (RET_HEADS, c_len, RET_HEAD_DIM))
    return {
        "ret_cos": jnp.concatenate([cos_r, cos_r], axis=-1),
        "ret_sin": jnp.concatenate([-sin_r, sin_r], axis=-1),
        "ret_decay": decay,
        "ret_wstate": bcast(w_state),
        "ret_wcross": bcast(w_cross),
        "ret_gchunk": jnp.broadcast_to(g_chunk[:, None, None], (RET_HEADS, RET_HEAD_DIM, RET_HEAD_DIM)),
        "moba_cos": jnp.tile(cos_m, (1, 2 * LANES // MOBA_HEAD_DIM)),
        "moba_sin": jnp.tile(jnp.concatenate([-sin_m, sin_m], axis=-1), (1, LANES // MOBA_HEAD_DIM)),
    }


def kernel(x, g_mix, w_in, conv_w, conv_b, conv_ln_g, conv_ln_b, q_norm_g, k_norm_g, w_ret_o, w_conv_o,
           w_moba_o, w_out, g_ffn, w_ff_gate, w_ff_up, w_ff_down, w_router, w_e_gate, w_e_up, w_e_down):
    batch, seq, d = x.shape
    depth = g_mix.shape[0]
    assert d == D_MODEL and w_in.shape[2] == IN_COLS
    xf = x.reshape(batch * seq, d)
    tabs = _tables(seq)
    w_ret_o, w_conv_o, w_moba_o, w_out, w_ff_gate, w_ff_up, w_ff_down, w_e_gate, w_e_up, w_e_down = (
        w.astype(BF16) for w in (w_ret_o, w_conv_o, w_moba_o, w_out, w_ff_gate, w_ff_up, w_ff_down,
                                 w_e_gate, w_e_up, w_e_down))
    hn = _rmsnorm(xf, g_mix[0])
    for l in range(depth):
        next_gain = g_mix[l + 1] if l + 1 < depth else jnp.ones((d,), F32)
        proj = _inproj(hn, w_in, l)
        yr = _retention(proj, tabs, batch, seq)
        yc = _conv_branch(proj, conv_w[l], conv_b[l], conv_ln_g[l], conv_ln_b[l], batch, seq)
        ym = _moba(proj, q_norm_g[l], k_norm_g[l], tabs, batch, seq)
        j = l // 2
        if l % 2 == 0:
            xf, h = _merge(xf, proj, yr, yc, ym, w_ret_o, w_conv_o, w_moba_o, w_out, g_ffn[l], None, l)
            xf, hn = _dense_ffn(xf, h, w_ff_gate, w_ff_up, w_ff_down, j, next_gain)
        else:
            xf, hp, logits = _merge(xf, proj, yr, yc, ym, w_ret_o, w_conv_o, w_moba_o, w_out, g_ffn[l],
                                    w_router[j], l)
            xf, hn = _moe(xf, hp, logits[:, :N_EXPERTS], w_e_gate, w_e_up, w_e_down, j, next_gain)
    return xf.reshape(batch, seq, d)
```

```python
import functools

import jax
import jax.numpy as jnp
from jax import lax
from jax.experimental import pallas as pl
from jax.experimental.pallas import tpu as pltpu

F32 = jnp.float32
BF16 = jnp.bfloat16

D_MODEL = 1024
RET_HEADS = 4
RET_HEAD_DIM = 128
RET_W = RET_HEADS * RET_HEAD_DIM
RET_CHUNK = 128
CONV_CH = 512
CONV_WIDTH = 31
MOBA_HEADS = 8
MOBA_HEAD_DIM = 64
MOBA_W = MOBA_HEADS * MOBA_HEAD_DIM
MOBA_BLOCK = 256
MOBA_TOPK = 3
ROPE_THETA = 10000.0
N_EXPERTS = 8
TOP_K = 2
NORM_EPS = 1e-6
GROUP_NORM_EPS = 1e-5

OFF_RQ, OFF_RK, OFF_RV, OFF_RG = 0, 512, 1024, 1536
OFF_CA, OFF_CB = 2048, 2560
OFF_MQ, OFF_MK, OFF_MV = 3072, 3584, 4096
OFF_GATES = 4608
IN_COLS = OFF_GATES + 3 * D_MODEL

LANES = 128
SUBLANES = 8
CONV_HALO = 32
VMEM_LIMIT_BYTES = 56 * 1024 * 1024
NEG_BIG = -1e30
LOG2_E = 1.4426950408889634

_NT = (((1,), (1,)), ((), ()))
_TN = (((0,), (0,)), ((), ()))


def _cparams(*sem):
    return pltpu.CompilerParams(dimension_semantics=sem, vmem_limit_bytes=VMEM_LIMIT_BYTES)


def _sigmoid(x):
    return 1.0 / (1.0 + jnp.exp(-x))


def _pack_bf16_pairs(x):
    w = x.shape[1] // 2
    bits = pltpu.bitcast(x.astype(BF16).astype(F32), jnp.uint32)
    return lax.shift_right_logical(bits[:, :w], jnp.uint32(16)) | bits[:, w:]


def _unpack_bf16_pairs(words):
    left = pltpu.bitcast(lax.shift_left(words, jnp.uint32(16)), F32)
    right = pltpu.bitcast(words & jnp.uint32(0xFFFF0000), F32)
    return left.astype(BF16), right.astype(BF16)


def _rms_norm(x, gain):
    ms = jnp.mean(x * x, axis=-1, keepdims=True)
    return x * lax.rsqrt(ms + NORM_EPS) * gain


def _rmsnorm_kernel(x_ref, g_ref, o_ref):
    o_ref[...] = _rms_norm(x_ref[...], g_ref[...]).astype(o_ref.dtype)


def _rmsnorm(xf, gain):
    n, d = xf.shape
    tm = 1024
    return pl.pallas_call(
        _rmsnorm_kernel,
        grid=(n // tm,),
        in_specs=[pl.BlockSpec((tm, d), lambda i: (i, 0)), pl.BlockSpec((1, d), lambda i: (0, 0))],
        out_specs=pl.BlockSpec((tm, d), lambda i: (i, 0)),
        out_shape=jax.ShapeDtypeStruct((n, d), BF16),
        compiler_params=_cparams("parallel"),
        name="rmsnorm",
    )(xf, gain.reshape(1, d))


def _inproj_kernel(h_ref, w_ref, o_ref, wb_ref):
    @pl.when(pl.program_id(1) == 0)
    def _():
        wb_ref[...] = w_ref[0].astype(BF16)

    o_ref[...] = jnp.dot(h_ref[...], wb_ref[...], preferred_element_type=F32).astype(o_ref.dtype)


def _inproj(h, w_in, layer):
    n, d = h.shape
    c = w_in.shape[2]
    tm, tn = 1024, 1536
    return pl.pallas_call(
        _inproj_kernel,
        grid=(c // tn, n // tm),
        in_specs=[pl.BlockSpec((tm, d), lambda j, i: (i, 0)),
                  pl.BlockSpec((1, d, tn), lambda j, i: (layer, 0, j))],
        out_specs=pl.BlockSpec((tm, tn), lambda j, i: (i, j)),
        out_shape=jax.ShapeDtypeStruct((n, c), BF16),
        scratch_shapes=[pltpu.VMEM((d, tn), BF16)],
        compiler_params=_cparams("arbitrary", "arbitrary"),
        name="inproj",
    )(h, w_in)


def _ret_kernel(q_ref, k_ref, v_ref, g_ref, cos_ref, sin_ref, dec_ref, wst_ref, wcr_ref, gch_ref,
                o_ref, state_ref, *, n_chunks):
    c_len = RET_CHUNK
    hd = RET_HEAD_DIM
    state_ref[...] = jnp.zeros(state_ref.shape, F32)

    def body(c, carry):
        r = pl.ds(pl.multiple_of(c * c_len, c_len), c_len)
        cs = cos_ref[r, :]
        sn = sin_ref[r, :]
        for h in range(RET_HEADS):
            hs = slice(h * hd, (h + 1) * hd)
            q = q_ref[r, hs].astype(F32)
            k = k_ref[r, hs].astype(F32)
            q = q * cs + pltpu.roll(q, hd // 2, 1) * sn
            k = (k * cs + pltpu.roll(k, hd // 2, 1) * sn) * (hd ** -0.5)
            qb = q.astype(BF16)
            kb = k.astype(BF16)
            vb = v_ref[r, hs]
            state = state_ref[h]
            s = lax.dot_general(qb, kb, _NT, preferred_element_type=F32) * dec_ref[h]
            y = jnp.dot(s.astype(BF16), vb, preferred_element_type=F32)
            y = y + jnp.dot(qb, state.astype(BF16), preferred_element_type=F32) * wcr_ref[h]
            kw = (k * wst_ref[h]).astype(BF16)
            kv = lax.dot_general(kw, vb, _TN, preferred_element_type=F32)
            state_ref[h] = gch_ref[h] * state + kv
            mu = jnp.mean(y, axis=-1, keepdims=True)
            yc = y - mu
            var = jnp.mean(yc * yc, axis=-1, keepdims=True)
            yn = yc * lax.rsqrt(var + GROUP_NORM_EPS)
            g = g_ref[r, hs].astype(F32)
            o_ref[r, hs] = (g * _sigmoid(g) * yn).astype(o_ref.dtype)
        return carry

    lax.fori_loop(0, n_chunks, body, 0, unroll=4)


def _retention(proj, tabs, batch, seq):
    n = proj.shape[0]
    hd = RET_HEAD_DIM

    def col(off):
        return pl.BlockSpec((seq, RET_W), lambda b, off=off: (b, off // RET_W))

    tab = pl.BlockSpec((seq, hd), lambda b: (0, 0))
    htab = pl.BlockSpec((RET_HEADS, hd, hd), lambda b: (0, 0, 0))
    return pl.pallas_call(
        functools.partial(_ret_kernel, n_chunks=seq // RET_CHUNK),
        grid=(batch,),
        in_specs=[col(OFF_RQ), col(OFF_RK), col(OFF_RV), col(OFF_RG), tab, tab, htab, htab, htab, htab],
        out_specs=pl.BlockSpec((seq, RET_W), lambda b: (b, 0)),
        out_shape=jax.ShapeDtypeStruct((n, RET_W), BF16),
        scratch_shapes=[pltpu.VMEM((RET_HEADS, hd, hd), F32)],
        compiler_params=_cparams("parallel"),
        name="retention",
    )(proj, proj, proj, proj, tabs["ret_cos"], tabs["ret_sin"], tabs["ret_decay"], tabs["ret_wstate"],
      tabs["ret_wcross"], tabs["ret_gchunk"])


def _conv_kernel(ca_ref, cb_ref, w_ref, b_ref, lg_ref, lb_ref, o_ref, ext_ref, sh_ref, *, t_tile, r_chunk):
    t = pl.program_id(1)

    @pl.when(t == 0)
    def _():
        ext_ref[0:CONV_HALO, :] = jnp.zeros((CONV_HALO, CONV_CH), F32)

    @pl.when(t > 0)
    def _():
        ext_ref[0:CONV_HALO, :] = ext_ref[t_tile:t_tile + CONV_HALO, :]

    ca = ca_ref[...].astype(F32)
    cb = cb_ref[...].astype(F32)
    ext_ref[CONV_HALO:, :] = ca * _sigmoid(cb)

    first_tap = CONV_HALO - (CONV_WIDTH - 1)
    shifted_rows = sh_ref.shape[1]
    for s in range(1, SUBLANES):
        sh_ref[s] = ext_ref[s:s + shifted_rows, :]
    bias = b_ref[...]
    lg = lg_ref[...]
    lb = lb_ref[...]
    for r0 in range(0, t_tile, r_chunk):
        acc = jnp.zeros((r_chunk, CONV_CH), F32) + bias
        for j in range(CONV_WIDTH):
            off = first_tap + j
            s = off % SUBLANES
            rows = pl.ds(r0 + off - s, r_chunk)
            tap = ext_ref[rows, :] if s == 0 else sh_ref[s, rows, :]
            acc = acc + tap * w_ref[j:j + 1, :]
        mu = jnp.mean(acc, axis=-1, keepdims=True)
        yc = acc - mu
        var = jnp.mean(yc * yc, axis=-1, keepdims=True)
        y = yc * lax.rsqrt(var + NORM_EPS) * lg + lb
        o_ref[r0:r0 + r_chunk, :] = (y * _sigmoid(y)).astype(o_ref.dtype)


def _conv_branch(proj, conv_w, conv_b, ln_g, ln_b, batch, seq):
    n = proj.shape[0]
    t_tile = 512
    tiles = seq // t_tile
    w = jnp.zeros((CONV_HALO, CONV_CH), F32).at[:CONV_WIDTH].set(conv_w)
    vec = pl.BlockSpec((1, CONV_CH), lambda b, t: (0, 0))

    def col(off):
        return pl.BlockSpec((t_tile, CONV_CH), lambda b, t, off=off: (b * tiles + t, off // CONV_CH))

    return pl.pallas_call(
        functools.partial(_conv_kernel, t_tile=t_tile, r_chunk=64),
        grid=(batch, tiles),
        in_specs=[col(OFF_CA), col(OFF_CB), pl.BlockSpec((CONV_HALO, CONV_CH), lambda b, t: (0, 0)),
                  vec, vec, vec],
        out_specs=pl.BlockSpec((t_tile, CONV_CH), lambda b, t: (b * tiles + t, 0)),
        out_shape=jax.ShapeDtypeStruct((n, CONV_CH), BF16),
        scratch_shapes=[pltpu.VMEM((CONV_HALO + t_tile, CONV_CH), F32),
                        pltpu.VMEM((SUBLANES, CONV_HALO + t_tile - SUBLANES, CONV_CH), F32)],
        compiler_params=_cparams("parallel", "arbitrary"),
        name="conv_branch",
    )(proj, proj, w, conv_b.reshape(1, -1), ln_g.reshape(1, -1), ln_b.reshape(1, -1))


def _moba_kernel(q_ref, k_ref, v_ref, qg_ref, kg_ref, cos_ref, sin_ref, o_ref, *scratch, n_blocks):
    def pair(p, carry):
        pc = pl.ds(pl.multiple_of(p * LANES, LANES), LANES)
        _moba_pair(pc, q_ref, k_ref, v_ref, qg_ref, kg_ref, cos_ref, sin_ref, o_ref, *scratch, n_blocks=n_blocks)
        return carry

    lax.fori_loop(0, q_ref.shape[1] // LANES, pair, 0)


def _moba_pair(pc, q_ref, k_ref, v_ref, qg_ref, kg_ref, cos_ref, sin_ref, o_ref,
               qf_s, q0_s, q1_s, k0_s, k1_s, km_s, *, n_blocks):
    bs = MOBA_BLOCK
    hd = MOBA_HEAD_DIM
    lane = lax.broadcasted_iota(jnp.int32, (1, LANES), 1)
    head0 = lane < hd
    first_half = (lane % hd) < (hd // 2)
    q_scale = (hd ** -0.5) * LOG2_E
    qg = qg_ref[...]
    kg = kg_ref[...]
    same_head = ((lax.broadcasted_iota(jnp.int32, (LANES, LANES), 0) < hd)
                 == (lax.broadcasted_iota(jnp.int32, (LANES, LANES), 1) < hd))
    head_mean = jnp.where(same_head, 1.0 / hd, 0.0).astype(BF16)

    def norm_rope(x, g, cs, sn):
        x2 = x * x
        hi = x2.astype(BF16)
        lo = (x2 - hi.astype(F32)).astype(BF16)
        ms = (jnp.dot(hi, head_mean, preferred_element_type=F32)
              + jnp.dot(lo, head_mean, preferred_element_type=F32))
        y = x * lax.rsqrt(ms + NORM_EPS) * g
        partner = jnp.where(first_half, pltpu.roll(y, LANES - hd // 2, 1), pltpu.roll(y, hd // 2, 1))
        return y * cs + partner * sn

    km_s[...] = jnp.zeros(km_s.shape, F32)

    def prologue(j, carry):
        r = pl.ds(pl.multiple_of(j * bs, bs), bs)
        cs = cos_ref[r, :]
        sn = sin_ref[r, :]
        qn = norm_rope(q_ref[r, pc].astype(F32), qg, cs, sn)
        kn = norm_rope(k_ref[r, pc].astype(F32), kg, cs, sn)
        qf_s[r, :] = qn
        qsc = qn * q_scale
        q0_s[r, :] = jnp.where(head0, qsc, 0.0).astype(BF16)
        q1_s[r, :] = jnp.where(head0, 0.0, qsc).astype(BF16)
        k0_s[r, :] = jnp.where(head0, kn, jnp.where(lane == hd + j, 1.0, 0.0)).astype(BF16)
        k1_s[r, :] = jnp.where(head0, jnp.where(lane == j, 1.0, 0.0), kn).astype(BF16)
        kmean = jnp.sum(kn, axis=0, keepdims=True) * (1.0 / bs)
        km_s[pl.ds(hd + j, 1), :] = jnp.where(head0, kmean, 0.0)
        km_s[pl.ds(j, 1), :] = jnp.where(head0, 0.0, kmean)
        return carry

    lax.fori_loop(0, n_blocks, prologue, 0, unroll=2)

    causal =(lax.broadcasted_iota(jnp.int32, (bs, bs), 0) >= lax.broadcasted_iota(jnp.int32, (bs, bs), 1))
    sub = lax.broadcasted_iota(jnp.int32, (8, bs), 0)

    def bias_rows(grp, b):
        rank = jnp.zeros(grp.shape, F32)
        for jp in range(b):
            row = grp[jp:jp + 1, :]
            ahead = (row > grp) | ((row == grp) & (jp < sub))
            rank = rank + jnp.where(ahead, 1.0, 0.0)
        return jnp.where((rank < float(MOBA_TOPK)) | (sub >= b), 0.0, NEG_BIG)

    for b in range(n_blocks):
        rq = slice(b * bs, (b + 1) * bs)
        past = b * bs
        if b > MOBA_TOPK:
            gate_t = lax.dot_general(km_s[...], qf_s[rq, :], _NT, precision=lax.Precision.HIGHEST,
                                     preferred_element_type=F32)
            pad = jnp.zeros((hd - 8, bs), F32)
            bias_t = jnp.concatenate([bias_rows(gate_t[0:8, :], b), pad,
                                      bias_rows(gate_t[hd:hd + 8, :], b), pad], axis=0)
            bias = bias_t.T
            qsc = qf_s[rq, :] * q_scale
            q_aug = (jnp.where(head0, qsc, bias).astype(BF16), jnp.where(head0, bias, qsc).astype(BF16))
        else:
            q_aug = (q0_s[rq, :], q1_s[rq, :])
        outs = []
        for h in range(2):
            ks = (k0_s, k1_s)[h]
            s_d = lax.dot_general(q_aug[h], ks[rq, :], _NT, preferred_element_type=F32)
            s_d = jnp.where(causal, s_d, NEG_BIG)
            m = jnp.max(s_d, axis=-1, keepdims=True)
            if past:
                s_p = lax.dot_general(q_aug[h], ks[0:past, :], _NT, preferred_element_type=F32)
                m = jnp.maximum(m, jnp.max(s_p, axis=-1, keepdims=True))
            p_d = jnp.exp2(s_d - m)
            l = jnp.sum(p_d, axis=-1, keepdims=True)
            acc = jnp.dot(p_d.astype(BF16), v_ref[rq, pc], preferred_element_type=F32)
            if past:
                p_p = jnp.exp2(s_p - m)
                l = l + jnp.sum(p_p, axis=-1, keepdims=True)
                acc = acc + jnp.dot(p_p.astype(BF16), v_ref[0:past, pc], preferred_element_type=F32)
            outs.append(acc / l)
        o_ref[rq, pc] = jnp.where(head0, outs[0], outs[1]).astype(o_ref.dtype)


def _moba(proj, q_gain, k_gain, tabs, batch, seq):
    n = proj.shape[0]
    assert seq % MOBA_BLOCK == 0
    n_blocks = seq // MOBA_BLOCK
    assert n_blocks <= 8

    def col(off):
        return pl.BlockSpec((seq, MOBA_W), lambda b, off=off: (b, off // MOBA_W))

    tab = pl.BlockSpec((seq, LANES), lambda b: (0, 0))
    vec = pl.BlockSpec((1, LANES), lambda b: (0, 0))
    return pl.pallas_call(
        functools.partial(_moba_kernel, n_blocks=n_blocks),
        grid=(batch,),
        in_specs=[col(OFF_MQ), col(OFF_MK), col(OFF_MV), vec, vec, tab, tab],
        out_specs=pl.BlockSpec((seq, MOBA_W), lambda b: (b, 0)),
        out_shape=jax.ShapeDtypeStruct((n, MOBA_W), BF16),
        scratch_shapes=[pltpu.VMEM((seq, LANES), F32), pltpu.VMEM((seq, LANES), BF16),
                        pltpu.VMEM((seq, LANES), BF16), pltpu.VMEM((seq, LANES), BF16),
                        pltpu.VMEM((seq, LANES), BF16), pltpu.VMEM((LANES, LANES), F32)],
        compiler_params=_cparams("parallel"),
        name="moba",
    )(proj, proj, proj, jnp.tile(q_gain, 2).reshape(1, LANES), jnp.tile(k_gain, 2).reshape(1, LANES),
      tabs["moba_cos"], tabs["moba_sin"])


def _merge_kernel(x_ref, ga_ref, gb_ref, yr_ref, yc_ref, ym_ref, wr_ref, wc_ref, wm_ref, wo_ref, gf_ref,
                  *rest, with_router):
    if with_router:
        wcat_ref, xo_ref, h_ref, lg_ref = rest
    else:
        xo_ref, h_ref = rest
    d = D_MODEL
    ga = ga_ref[...].astype(F32)
    gb = gb_ref[...].astype(F32)
    g_ret = ga[:, :d]
    g_conv = jnp.concatenate([ga[:, d:], gb[:, :d // 2]], axis=-1)
    g_moba = gb[:, d // 2:]
    merged = (_sigmoid(g_ret) * jnp.dot(yr_ref[...], wr_ref[0], preferred_element_type=F32)
              + _sigmoid(g_conv) * jnp.dot(yc_ref[...], wc_ref[0], preferred_element_type=F32)
              + _sigmoid(g_moba) * jnp.dot(ym_ref[...], wm_ref[0], preferred_element_type=F32))
    x = x_ref[...] + jnp.dot(merged.astype(BF16), wo_ref[0], preferred_element_type=F32)
    xo_ref[...] = x
    h = _rms_norm(x, gf_ref[...])
    if with_router:
        h_ref[...] = _pack_bf16_pairs(h)
        h_hi = h.astype(BF16)
        h_lo = (h - h_hi.astype(F32)).astype(BF16)
        both = jnp.dot(h_hi, wcat_ref[...], preferred_element_type=F32)
        lg_ref[...] = (both[:, :LANES] + both[:, LANES:]
                       + jnp.dot(h_lo, wcat_ref[:, :LANES], preferred_element_type=F32))
    else:
        h_ref[...] = h.astype(BF16)


def _merge(xf, proj, yr, yc, ym, w_ret_o, w_conv_o, w_moba_o, w_out, g_ffn, w_router, layer):
    n, d = xf.shape
    tm = 512
    gw = 3 * d // 2
    with_router = w_router is not None
    row = lambda w: pl.BlockSpec((tm, w), lambda i: (i, 0))
    full = lambda a: pl.BlockSpec(a.shape, lambda i: (0,) * a.ndim)
    of_layer = lambda a: pl.BlockSpec((1,) + a.shape[1:], lambda i: (layer,) + (0,) * (a.ndim - 1))
    gf = g_ffn.reshape(1, d)
    args = [xf, proj, proj, yr, yc, ym, w_ret_o, w_conv_o, w_moba_o, w_out, gf]
    in_specs = [row(d),
                pl.BlockSpec((tm, gw), lambda i: (i, OFF_GATES // gw)),
                pl.BlockSpec((tm, gw), lambda i: (i, OFF_GATES // gw + 1)),
                row(RET_W), row(CONV_CH), row(MOBA_W), of_layer(w_ret_o), of_layer(w_conv_o),
                of_layer(w_moba_o), of_layer(w_out), full(gf)]
    if with_router:
        out_shape = [jax.ShapeDtypeStruct((n, d), F32), jax.ShapeDtypeStruct((n, d // 2), jnp.uint32)]
        out_specs = [row(d), row(d // 2)]
    else:
        out_shape = [jax.ShapeDtypeStruct((n, d), F32), jax.ShapeDtypeStruct((n, d), BF16)]
        out_specs = [row(d), row(d)]
    if with_router:
        wrt = jnp.zeros((d, LANES), F32).at[:, :N_EXPERTS].set(w_router)
        w_hi = wrt.astype(BF16)
        w_lo = (wrt - w_hi.astype(F32)).astype(BF16)
        w_cat = jnp.concatenate([w_hi, w_lo], axis=1)
        args.append(w_cat)
        in_specs.append(full(w_cat))
        out_shape.append(jax.ShapeDtypeStruct((n, LANES), F32))
        out_specs.append(row(LANES))
    return pl.pallas_call(
        functools.partial(_merge_kernel, with_router=with_router),
        grid=(n // tm,),
        in_specs=in_specs, out_specs=out_specs, out_shape=out_shape,
        compiler_params=_cparams("parallel"),
        name="merge",
    )(*args)


def _ffn_kernel(x_ref, h_ref, wg_ref, wu_ref, wd_ref, gn_ref, o_ref, hn_ref):
    f = pl.program_id(1)
    h = h_ref[...]
    g = jnp.dot(h, wg_ref[0], preferred_element_type=F32)
    u = jnp.dot(h, wu_ref[0], preferred_element_type=F32)
    a = (g * _sigmoid(g) * u).astype(BF16)
    dlt = jnp.dot(a, wd_ref[0], preferred_element_type=F32)

    @pl.when(f == 0)
    def _():
        o_ref[...] = x_ref[...] + dlt

    @pl.when(f > 0)
    def _():
        o_ref[...] += dlt

    @pl.when(f == pl.num_programs(1) - 1)
    def _():
        hn_ref[...] = _rms_norm(o_ref[...], gn_ref[...]).astype(hn_ref.dtype)


def _dense_ffn(xf, h, w_gate, w_up, w_down, layer, next_gain):
    n, d = xf.shape
    ff = w_gate.shape[2]
    tm, tf = 512, 1408
    assert ff % tf == 0
    return pl.pallas_call(
        _ffn_kernel,
        grid=(n // tm, ff // tf),
        in_specs=[pl.BlockSpec((tm, d), lambda i, f: (i, 0)),
                  pl.BlockSpec((tm, d), lambda i, f: (i, 0)),
                  pl.BlockSpec((1, d, tf), lambda i, f: (layer, 0, f)),
                  pl.BlockSpec((1, d, tf), lambda i, f: (layer, 0, f)),
                  pl.BlockSpec((1, tf, d), lambda i, f: (layer, f, 0)),
                  pl.BlockSpec((1, d), lambda i, f: (0, 0))],
        out_specs=[pl.BlockSpec((tm, d), lambda i, f: (i, 0)), pl.BlockSpec((tm, d), lambda i, f: (i, 0))],
        out_shape=[jax.ShapeDtypeStruct((n, d), F32), jax.ShapeDtypeStruct((n, d), BF16)],
        compiler_params=_cparams("parallel", "arbitrary"),
        name="dense_ffn",
    )(xf, h, w_gate, w_up, w_down, next_gain.reshape(1, d))


def _row_copies(n_groups, group_copies):
    def start(g, carry):
        for cp in group_copies(g):
            cp.start()
        return carry

    def wait(g, carry):
        for cp in group_copies(0):
            cp.wait()
        return carry

    lax.fori_loop(0, n_groups, start, 0, unroll=2)
    lax.fori_loop(0, n_groups, wait, 0, unroll=2)


def _dispatch_kernel(dest_ref, hp_ref, xz_ref, xbuf_ref, sem, *, n_tok):
    del xz_ref

    def group(g):
        tok0 = pl.multiple_of(g * SUBLANES, SUBLANES)
        return [pltpu.make_async_copy(hp_ref.at[pl.ds(tok0 + r, 1), :],
                                      xbuf_ref.at[pl.ds(dest_ref[0, 0, TOP_K * (tok0 + r) + k], 1), :], sem)
                for r in range(SUBLANES) for k in range(TOP_K)]

    _row_copies(n_tok // SUBLANES, group)


def _dispatch(hp, dest3, p_rows):
    n, w = hp.shape
    steps, _, per_step = dest3.shape
    n_tok = per_step // TOP_K
    return pl.pallas_call(
        functools.partial(_dispatch_kernel, n_tok=n_tok),
        grid=(steps,),
        in_specs=[pl.BlockSpec((1, 1, per_step), lambda i: (i, 0, 0), memory_space=pltpu.SMEM),
                  pl.BlockSpec((n_tok, w), lambda i: (i, 0)),
                  pl.BlockSpec(memory_space=pl.ANY)],
        out_specs=pl.BlockSpec(memory_space=pl.ANY),
        out_shape=jax.ShapeDtypeStruct((p_rows, w), hp.dtype),
        scratch_shapes=[pltpu.SemaphoreType.DMA(())],
        input_output_aliases={2: 0},
        compiler_params=_cparams("arbitrary"),
        name="moe_dispatch",
    )(dest3, hp, jnp.zeros((p_rows, w), hp.dtype))


def _moe_kernel(te_ref, act_ref, x_ref, wg_ref, wu_ref, wd_ref, o_ref, xs_ref):
    i = pl.program_id(0)
    f = pl.program_id(1)
    active = act_ref[i] > 0
    half = xs_ref.shape[1] // 2

    @pl.when(active & (f == 0))
    def _():
        left, right = _unpack_bf16_pairs(x_ref[...])
        xs_ref[:, :half] = left
        xs_ref[:, half:] = right

    @pl.when(active)
    def _():
        x = xs_ref[...]
        g = jnp.dot(x, wg_ref[0, 0], preferred_element_type=F32)
        u = jnp.dot(x, wu_ref[0, 0], preferred_element_type=F32)
        a = (g * _sigmoid(g) * u).astype(BF16)
        dlt = jnp.dot(a, wd_ref[0, 0], preferred_element_type=F32)

        @pl.when(f == 0)
        def _():
            o_ref[...] = dlt

        @pl.when(f > 0)
        def _():
            o_ref[...] += dlt

    @pl.when(jnp.logical_not(active) & (f == 0))
    def _():
        o_ref[...] = jnp.zeros(o_ref.shape, o_ref.dtype)


def _moe_experts(xbuf, tile_expert, tile_active, w_gate, w_up, w_down, layer, tm):
    p, half = xbuf.shape
    d = 2 * half
    ff = w_gate.shape[3]
    tf = 1792
    assert ff % tf == 0
    nf = ff // tf

    def fsel(f, act, i):
        return jnp.where(act[i] > 0, f, nf - 1)

    grid_spec = pltpu.PrefetchScalarGridSpec(
        num_scalar_prefetch=2,
        grid=(p // tm, nf),
        in_specs=[pl.BlockSpec((tm, half), lambda i, f, te, act: (i, 0)),
                  pl.BlockSpec((1, 1, d, tf), lambda i, f, te, act: (layer, te[i], 0, fsel(f, act, i))),
                  pl.BlockSpec((1, 1, d, tf), lambda i, f, te, act: (layer, te[i], 0, fsel(f, act, i))),
                  pl.BlockSpec((1, 1, tf, d), lambda i, f, te, act: (layer, te[i], fsel(f, act, i), 0))],
        out_specs=pl.BlockSpec((tm, d), lambda i, f, te, act: (i, 0)),
        scratch_shapes=[pltpu.VMEM((tm, d), BF16)],
    )
    return pl.pallas_call(
        _moe_kernel,
        grid_spec=grid_spec,
        out_shape=jax.ShapeDtypeStruct((p, d), F32),
        compiler_params=_cparams("arbitrary", "arbitrary"),
        name="moe_experts",
    )(tile_expert, tile_active, xbuf, w_gate, w_up, w_down)


def _combine_kernel(dest_ref, x_ref, g_ref, gn_ref, ybuf_ref, o_ref, hn_ref, ys_ref, sem, *, n_tok):
    def group(g):
        tok0 = pl.multiple_of(g * SUBLANES, SUBLANES)
        return [pltpu.make_async_copy(ybuf_ref.at[pl.ds(dest_ref[0, 0, TOP_K * (tok0 + r) + k], 1), :],
                                      ys_ref.at[pl.ds(k * n_tok + tok0 + r, 1), :], sem)
                for r in range(SUBLANES) for k in range(TOP_K)]

    _row_copies(n_tok // SUBLANES, group)
    g = g_ref[...]
    y = g[:, 0:1] * ys_ref[0:n_tok, :] + g[:, 1:2] * ys_ref[n_tok:, :]
    out = x_ref[...] + y
    o_ref[...] = out
    hn_ref[...] = _rms_norm(out, gn_ref[...]).astype(hn_ref.dtype)


def _combine(xf, gates, dest3, ybuf, next_gain):
    n, d = xf.shape
    steps, _, per_step = dest3.shape
    n_tok = per_step // TOP_K
    assert TOP_K == 2
    row = pl.BlockSpec((n_tok, d), lambda i: (i, 0))
    return pl.pallas_call(
        functools.partial(_combine_kernel, n_tok=n_tok),
        grid=(steps,),
        in_specs=[pl.BlockSpec((1, 1, per_step), lambda i: (i, 0, 0), memory_space=pltpu.SMEM),
                  row,
                  pl.BlockSpec((n_tok, TOP_K), lambda i: (i, 0)),
                  pl.BlockSpec((1, d), lambda i: (0, 0)),
                  pl.BlockSpec(memory_space=pl.ANY)],
        out_specs=[row, row],
        out_shape=[jax.ShapeDtypeStruct((n, d), F32), jax.ShapeDtypeStruct((n, d), BF16)],
        scratch_shapes=[pltpu.VMEM((TOP_K * n_tok, d), F32), pltpu.SemaphoreType.DMA(())],
        compiler_params=_cparams("arbitrary"),
        name="moe_combine",
    )(dest3, xf, gates, next_gain.reshape(1, d), ybuf)


def _moe(xf, hp, logits, w_gate, w_up, w_down, layer, next_gain):
    n, d = xf.shape
    na = n * TOP_K
    tm = 512
    n_tok = 512
    n_tiles = na // tm + N_EXPERTS
    top_vals, top_idx = lax.top_k(logits, TOP_K)
    gates = jax.nn.softmax(top_vals, axis=-1)
    e_flat = top_idx.reshape(-1)
    onehot = (e_flat[:, None] == jnp.arange(N_EXPERTS, dtype=e_flat.dtype)[None, :]).astype(jnp.int32)
    csum = jnp.cumsum(onehot, axis=0)
    pos_in_e = jnp.sum(csum * onehot, axis=1) - 1
    counts = csum[-1]
    tiles_per_e = (counts + tm - 1) // tm
    tile_end = jnp.cumsum(tiles_per_e)
    tile_start = tile_end - tiles_per_e
    dest = jnp.sum(tile_start[None, :] * onehot, axis=1) * tm + pos_in_e
    tile_ids = jnp.arange(n_tiles, dtype=jnp.int32)
    tile_expert = jnp.minimum(jnp.sum((tile_end[None, :] <= tile_ids[:, None]).astype(jnp.int32), axis=1),
                              N_EXPERTS - 1).astype(jnp.int32)
    tile_active = (tile_ids < tile_end[-1]).astype(jnp.int32)
    dest3 = dest.astype(jnp.int32).reshape(n // n_tok, 1, TOP_K * n_tok)
    xbuf = _dispatch(hp, dest3, n_tiles * tm)
    ybuf = _moe_experts(xbuf, tile_expert, tile_active, w_gate, w_up, w_down, layer, tm)
    return _combine(xf, gates, dest3, ybuf, next_gain)


def _rope(seq, dim):
    inv = 1.0 / (ROPE_THETA ** (jnp.arange(0, dim, 2, dtype=F32) / dim))
    ang = jnp.arange(seq, dtype=F32)[:, None] * inv[None, :]
    return jnp.cos(ang), jnp.sin(ang)


def _tables(seq):
    cos_r, sin_r = _rope(seq, RET_HEAD_DIM)
    cos_m, sin_m = _rope(seq, MOBA_HEAD_DIM)
    c_len = RET_CHUNK
    log_g = jnp.log(1.0 - jnp.exp2(-5.0 - jnp.arange(RET_HEADS, dtype=F32)))
    pos = jnp.arange(c_len, dtype=F32)
    diff = pos[:, None] - pos[None, :]
    decay = jnp.where(diff[None] >= 0, jnp.exp(diff[None] * log_g[:, None, None]), 0.0)
    w_state = jnp.exp((c_len - 1.0 - pos)[None, :] * log_g[:, None])
    w_cross = jnp.exp((pos + 1.0)[None, :] * log_g[:, None])
    g_chunk = jnp.exp(c_len * log_g)
    bcast = lambda a: jnp.broadcast_to(a[:, :, None], (RET_HEADS, c_len, RET_HEAD_DIM))
    return {
        "ret_cos": jnp.concatenate([cos_r, cos_r], axis=-1),
        "ret_sin": jnp.concatenate([-sin_r, sin_r], axis=-1),
        "ret_decay": decay,
        "ret_wstate": bcast(w_state),
        "ret_wcross": bcast(w_cross),
        "ret_gchunk": jnp.broadcast_to(g_chunk[:, None, None], (RET_HEADS, RET_HEAD_DIM, RET_HEAD_DIM)),
        "moba_cos": jnp.tile(cos_m, (1, 2 * LANES // MOBA_HEAD_DIM)),
        "moba_sin": jnp.tile(jnp.concatenate([-sin_m, sin_m], axis=-1), (1, LANES // MOBA_HEAD_DIM)),
    }


def kernel(x, g_mix, w_in, conv_w, conv_b, conv_ln_g, conv_ln_b, q_norm_g, k_norm_g, w_ret_o, w_conv_o,
           w_moba_o, w_out, g_ffn, w_ff_gate, w_ff_up, w_ff_down, w_router, w_e_gate, w_e_up, w_e_down):
    batch, seq, d = x.shape
    depth = g_mix.shape[0]
    assert d == D_MODEL and w_in.shape[2] == IN_COLS
    xf = x.reshape(batch * seq, d)
    tabs = _tables(seq)
    w_ret_o, w_conv_o, w_moba_o, w_out, w_ff_gate, w_ff_up, w_ff_down, w_e_gate, w_e_up, w_e_down = (
        w.astype(BF16) for w in (w_ret_o, w_conv_o, w_moba_o, w_out, w_ff_gate, w_ff_up, w_ff_down,
                                 w_e_gate, w_e_up, w_e_down))
    hn = _rmsnorm(xf, g_mix[0])
    for l in range(depth):
        next_gain = g_mix[l + 1] if l + 1 < depth else jnp.ones((d,), F32)
        proj = _inproj(hn, w_in, l)
        yr = _retention(proj, tabs, batch, seq)
        yc = _conv_branch(proj, conv_w[l], conv_b[l], conv_ln_g[l], conv_ln_b[l], batch, seq)
        ym = _moba(proj, q_norm_g[l], k_norm_g[l], tabs, batch, seq)
        j = l // 2
        if l % 2 == 0:
            xf, h = _merge(xf, proj, yr, yc, ym, w_ret_o, w_conv_o, w_moba_o, w_out, g_ffn[l], None, l)
            xf, hn = _dense_ffn(xf, h, w_ff_gate, w_ff_up, w_ff_down, j, next_gain)
        else:
            xf, hp, logits = _merge(xf, proj, yr, yc, ym, w_ret_o, w_conv_o, w_moba_o, w_out, g_ffn[l],
                                    w_router[j], l)
            xf, hn = _moe(xf, hp, logits[:, :N_EXPERTS], w_e_gate, w_e_up, w_e_down, j, next_gain)
    return xf.reshape(batch, seq, d)
```

```python
import functools

import jax
import jax.numpy as jnp
from jax import lax
from jax.experimental import pallas as pl
from jax.experimental.pallas import tpu as pltpu

F32 = jnp.float32
BF16 = jnp.bfloat16

D_MODEL = 1024
RET_HEADS = 4
RET_HEAD_DIM = 128
RET_W = RET_HEADS * RET_HEAD_DIM
RET_CHUNK = 128
CONV_CH = 512
CONV_WIDTH = 31
MOBA_HEADS = 8
MOBA_HEAD_DIM = 64
MOBA_W = MOBA_HEADS * MOBA_HEAD_DIM
MOBA_BLOCK = 256
MOBA_TOPK = 3
ROPE_THETA = 10000.0
N_EXPERTS = 8
TOP_K = 2
NORM_EPS = 1e-6
GROUP_NORM_EPS = 1e-5

OFF_RQ, OFF_RK, OFF_RV, OFF_RG = 0, 512, 1024, 1536
OFF_CA, OFF_CB = 2048, 2560
OFF_MQ, OFF_MK, OFF_MV = 3072, 3584, 4096
OFF_GATES = 4608
IN_COLS = OFF_GATES + 3 * D_MODEL

LANES = 128
SUBLANES = 8
CONV_HALO = 32
VMEM_LIMIT_BYTES = 56 * 1024 * 1024
NEG_BIG = -1e30
LOG2_E = 1.4426950408889634

_NT = (((1,), (1,)), ((), ()))
_TN = (((0,), (0,)), ((), ()))


def _cparams(*sem):
    return pltpu.CompilerParams(dimension_semantics=sem, vmem_limit_bytes=VMEM_LIMIT_BYTES)


def _sigmoid(x):
    return 1.0 / (1.0 + jnp.exp(-x))


def _pack_bf16_pairs(x):
    w = x.shape[1] // 2
    bits = pltpu.bitcast(x.astype(BF16).astype(F32), jnp.uint32)
    return lax.shift_right_logical(bits[:, :w], jnp.uint32(16)) | bits[:, w:]


def _unpack_bf16_pairs(words):
    left = pltpu.bitcast(lax.shift_left(words, jnp.uint32(16)), F32)
    right = pltpu.bitcast(words & jnp.uint32(0xFFFF0000), F32)
    return left.astype(BF16), right.astype(BF16)


def _rms_norm(x, gain):
    ms = jnp.mean(x * x, axis=-1, keepdims=True)
    return x * lax.rsqrt(ms + NORM_EPS) * gain


def _rmsnorm_kernel(x_ref, g_ref, o_ref):
    o_ref[...] = _rms_norm(x_ref[...], g_ref[...]).astype(o_ref.dtype)


def _rmsnorm(xf, gain):
    n, d = xf.shape
    tm = 1024
    return pl.pallas_call(
        _rmsnorm_kernel,
        grid=(n // tm,),
        in_specs=[pl.BlockSpec((tm, d), lambda i: (i, 0)), pl.BlockSpec((1, d), lambda i: (0, 0))],
        out_specs=pl.BlockSpec((tm, d), lambda i: (i, 0)),
        out_shape=jax.ShapeDtypeStruct((n, d), BF16),
        compiler_params=_cparams("parallel"),
        name="rmsnorm",
    )(xf, gain.reshape(1, d))


def _inproj_kernel(h_ref, w_ref, o_ref, wb_ref):
    @pl.when(pl.program_id(1) == 0)
    def _():
        wb_ref[...] = w_ref[0].astype(BF16)

    o_ref[...] = jnp.dot(h_ref[...], wb_ref[...], preferred_element_type=F32).astype(o_ref.dtype)


def _inproj(h, w_in, layer):
    n, d = h.shape
    c = w_in.shape[2]
    tm, tn = 1024, 2560
    return pl.pallas_call(
        _inproj_kernel,
        grid=(c // tn, n // tm),
        in_specs=[pl.BlockSpec((tm, d), lambda j, i: (i, 0)),
                  pl.BlockSpec((1, d, tn), lambda j, i: (layer, 0, j))],
        out_specs=pl.BlockSpec((tm, tn), lambda j, i: (i, j)),
        out_shape=jax.ShapeDtypeStruct((n, c), BF16),
        scratch_shapes=[pltpu.VMEM((d, tn), BF16)],
        compiler_params=_cparams("arbitrary", "arbitrary"),
        name="inproj",
    )(h, w_in)


def _ret_kernel(q_ref, k_ref, v_ref, g_ref, cos_ref, sin_ref, dec_ref, wst_ref, wcr_ref, gch_ref,
                o_ref, state_ref, *, n_chunks):
    c_len = RET_CHUNK
    hd = RET_HEAD_DIM
    state_ref[...] = jnp.zeros(state_ref.shape, F32)

    def body(c, carry):
        r = pl.ds(pl.multiple_of(c * c_len, c_len), c_len)
        cs = cos_ref[r, :]
        sn = sin_ref[r, :]
        for h in range(RET_HEADS):
            hs = slice(h * hd, (h + 1) * hd)
            q = q_ref[r, hs].astype(F32)
            k = k_ref[r, hs].astype(F32)
            q = q * cs + pltpu.roll(q, hd // 2, 1) * sn
            k = (k * cs + pltpu.roll(k, hd // 2, 1) * sn) * (hd ** -0.5)
            qb = q.astype(BF16)
            kb = k.astype(BF16)
            vb = v_ref[r, hs]
            state = state_ref[h]
            s = lax.dot_general(qb, kb, _NT, preferred_element_type=F32) * dec_ref[h]
            y = jnp.dot(s.astype(BF16), vb, preferred_element_type=F32)
            y = y + jnp.dot(qb, state.astype(BF16), preferred_element_type=F32) * wcr_ref[h]
            kw = (k * wst_ref[h]).astype(BF16)
            kv = lax.dot_general(kw, vb, _TN, preferred_element_type=F32)
            state_ref[h] = gch_ref[h] * state + kv
            mu = jnp.mean(y, axis=-1, keepdims=True)
            yc = y - mu
            var = jnp.mean(yc * yc, axis=-1, keepdims=True)
            yn = yc * lax.rsqrt(var + GROUP_NORM_EPS)
            g = g_ref[r, hs].astype(F32)
            o_ref[r, hs] = (g * _sigmoid(g) * yn).astype(o_ref.dtype)
        return carry

    lax.fori_loop(0, n_chunks, body, 0, unroll=8)


def _retention(proj, tabs, batch, seq):
    n = proj.shape[0]
    hd = RET_HEAD_DIM

    def col(off):
        return pl.BlockSpec((seq, RET_W), lambda b, off=off: (b, off // RET_W))

    tab = pl.BlockSpec((seq, hd), lambda b: (0, 0))
    htab = pl.BlockSpec((RET_HEADS, hd, hd), lambda b: (0, 0, 0))
    return pl.pallas_call(
        functools.partial(_ret_kernel, n_chunks=seq // RET_CHUNK),
        grid=(batch,),
        in_specs=[col(OFF_RQ), col(OFF_RK), col(OFF_RV), col(OFF_RG), tab, tab, htab, htab, htab, htab],
        out_specs=pl.BlockSpec((seq, RET_W), lambda b: (b, 0)),
        out_shape=jax.ShapeDtypeStruct((n, RET_W), BF16),
        scratch_shapes=[pltpu.VMEM((RET_HEADS, hd, hd), F32)],
        compiler_params=_cparams("parallel"),
        name="retention",
    )(proj, proj, proj, proj, tabs["ret_cos"], tabs["ret_sin"], tabs["ret_decay"], tabs["ret_wstate"],
      tabs["ret_wcross"], tabs["ret_gchunk"])


def _conv_kernel(ca_ref, cb_ref, w_ref, b_ref, lg_ref, lb_ref, o_ref, ext_ref, sh_ref, *, t_tile, r_chunk):
    t = pl.program_id(1)

    @pl.when(t == 0)
    def _():
        ext_ref[0:CONV_HALO, :] = jnp.zeros((CONV_HALO, CONV_CH), F32)

    @pl.when(t > 0)
    def _():
        ext_ref[0:CONV_HALO, :] = ext_ref[t_tile:t_tile + CONV_HALO, :]

    ca = ca_ref[...].astype(F32)
    cb = cb_ref[...].astype(F32)
    ext_ref[CONV_HALO:, :] = ca * _sigmoid(cb)

    first_tap = CONV_HALO - (CONV_WIDTH - 1)
    shifted_rows = sh_ref.shape[1]
    for s in range(1, SUBLANES):
        sh_ref[s] = ext_ref[s:s + shifted_rows, :]
    bias = b_ref[...]
    lg = lg_ref[...]
    lb = lb_ref[...]
    for r0 in range(0, t_tile, r_chunk):
        acc = jnp.zeros((r_chunk, CONV_CH), F32) + bias
        for j in range(CONV_WIDTH):
            off = first_tap + j
            s = off % SUBLANES
            rows = pl.ds(r0 + off - s, r_chunk)
            tap = ext_ref[rows, :] if s == 0 else sh_ref[s, rows, :]
            acc = acc + tap * w_ref[j:j + 1, :]
        mu = jnp.mean(acc, axis=-1, keepdims=True)
        yc = acc - mu
        var = jnp.mean(yc * yc, axis=-1, keepdims=True)
        y = yc * lax.rsqrt(var + NORM_EPS) * lg + lb
        o_ref[r0:r0 + r_chunk, :] = (y * _sigmoid(y)).astype(o_ref.dtype)


def _conv_branch(proj, conv_w, conv_b, ln_g, ln_b, batch, seq):
    n = proj.shape[0]
    t_tile = 512
    tiles = seq // t_tile
    w = jnp.zeros((CONV_HALO, CONV_CH), F32).at[:CONV_WIDTH].set(conv_w)
    vec = pl.BlockSpec((1, CONV_CH), lambda b, t: (0, 0))

    def col(off):
        return pl.BlockSpec((t_tile, CONV_CH), lambda b, t, off=off: (b * tiles + t, off // CONV_CH))

    return pl.pallas_call(
        functools.partial(_conv_kernel, t_tile=t_tile, r_chunk=64),
        grid=(batch, tiles),
        in_specs=[col(OFF_CA), col(OFF_CB), pl.BlockSpec((CONV_HALO, CONV_CH), lambda b, t: (0, 0)),
                  vec, vec, vec],
        out_specs=pl.BlockSpec((t_tile, CONV_CH), lambda b, t: (b * tiles + t, 0)),
        out_shape=jax.ShapeDtypeStruct((n, CONV_CH), BF16),
        scratch_shapes=[pltpu.VMEM((CONV_HALO + t_tile, CONV_CH), F32),
                        pltpu.VMEM((SUBLANES, CONV_HALO + t_tile - SUBLANES, CONV_CH), F32)],
        compiler_params=_cparams("parallel", "arbitrary"),
        name="conv_branch",
    )(proj, proj, w, conv_b.reshape(1, -1), ln_g.reshape(1, -1), ln_b.reshape(1, -1))


def _moba_kernel(q_ref, k_ref, v_ref, qg_ref, kg_ref, cos_ref, sin_ref, o_ref, *scratch, n_blocks):
    def pair(p, carry):
        pc = pl.ds(pl.multiple_of(p * LANES, LANES), LANES)
        _moba_pair(pc, q_ref, k_ref, v_ref, qg_ref, kg_ref, cos_ref, sin_ref, o_ref, *scratch, n_blocks=n_blocks)
        return carry

    lax.fori_loop(0, q_ref.shape[1] // LANES, pair, 0)


def _moba_pair(pc, q_ref, k_ref, v_ref, qg_ref, kg_ref, cos_ref, sin_ref, o_ref,
               qf_s, q0_s, q1_s, k0_s, k1_s, km_s, *, n_blocks):
    bs = MOBA_BLOCK
    hd = MOBA_HEAD_DIM
    lane = lax.broadcasted_iota(jnp.int32, (1, LANES), 1)
    head0 = lane < hd
    first_half = (lane % hd) < (hd // 2)
    q_scale = (hd ** -0.5) * LOG2_E
    qg = qg_ref[...]
    kg = kg_ref[...]
    same_head = ((lax.broadcasted_iota(jnp.int32, (LANES, LANES), 0) < hd)
                 == (lax.broadcasted_iota(jnp.int32, (LANES, LANES), 1) < hd))
    head_mean = jnp.where(same_head, 1.0 / hd, 0.0).astype(BF16)

    def norm_rope(x, g, cs, sn):
        x2 = x * x
        hi = x2.astype(BF16)
        lo = (x2 - hi.astype(F32)).astype(BF16)
        ms = (jnp.dot(hi, head_mean, preferred_element_type=F32)
              + jnp.dot(lo, head_mean, preferred_element_type=F32))
        y = x * lax.rsqrt(ms + NORM_EPS) * g
        partner = jnp.where(first_half, pltpu.roll(y, LANES - hd // 2, 1), pltpu.roll(y, hd // 2, 1))
        return y * cs + partner * sn

    km_s[...] = jnp.zeros(km_s.shape, F32)

    def prologue(j, carry):
        r = pl.ds(pl.multiple_of(j * bs, bs), bs)
        cs = cos_ref[r, :]
        sn = sin_ref[r, :]
        qn = norm_rope(q_ref[r, pc].astype(F32), qg, cs, sn)
        kn = norm_rope(k_ref[r, pc].astype(F32), kg, cs, sn)
        qf_s[r, :] = qn
        qsc = qn * q_scale
        q0_s[r, :] = jnp.where(head0, qsc, 0.0).astype(BF16)
        q1_s[r, :] = jnp.where(head0, 0.0, qsc).astype(BF16)
        k0_s[r, :] = jnp.where(head0, kn, jnp.where(lane == hd + j, 1.0, 0.0)).astype(BF16)
        k1_s[r, :] = jnp.where(head0, jnp.where(lane == j, 1.0, 0.0), kn).astype(BF16)
        kmean = jnp.sum(kn, axis=0, keepdims=True) * (1.0 / bs)
        km_s[pl.ds(hd + j, 1), :] = jnp.where(head0, kmean, 0.0)
        km_s[pl.ds(j, 1), :] = jnp.where(head0, 0.0, kmean)
        return carry

    lax.fori_loop(0, n_blocks, prologue, 0, unroll=4)

    causal =(lax.broadcasted_iota(jnp.int32, (bs, bs), 0) >= lax.broadcasted_iota(jnp.int32, (bs, bs), 1))
    sub = lax.broadcasted_iota(jnp.int32, (8, bs), 0)

    def bias_rows(grp, b):
        rank = jnp.zeros(grp.shape, F32)
        for jp in range(b):
            row = grp[jp:jp + 1, :]
            ahead = (row > grp) | ((row == grp) & (jp < sub))
            rank = rank + jnp.where(ahead, 1.0, 0.0)
        return jnp.where((rank < float(MOBA_TOPK)) | (sub >= b), 0.0, NEG_BIG)

    for b in range(n_blocks):
        rq = slice(b * bs, (b + 1) * bs)
        past = b * bs
        if b > MOBA_TOPK:
            gate_t = lax.dot_general(km_s[...], qf_s[rq, :], _NT, precision=lax.Precision.HIGHEST,
                                     preferred_element_type=F32)
            pad = jnp.zeros((hd - 8, bs), F32)
            bias_t = jnp.concatenate([bias_rows(gate_t[0:8, :], b), pad,
                                      bias_rows(gate_t[hd:hd + 8, :], b), pad], axis=0)
            bias = bias_t.T
            qsc = qf_s[rq, :] * q_scale
            q_aug = (jnp.where(head0, qsc, bias).astype(BF16), jnp.where(head0, bias, qsc).astype(BF16))
        else:
            q_aug = (q0_s[rq, :], q1_s[rq, :])
        outs = []
        for h in range(2):
            ks = (k0_s, k1_s)[h]
            s_d = lax.dot_general(q_aug[h], ks[rq, :], _NT, preferred_element_type=F32)
            s_d = jnp.where(causal, s_d, NEG_BIG)
            m = jnp.max(s_d, axis=-1, keepdims=True)
            if past:
                s_p = lax.dot_general(q_aug[h], ks[0:past, :], _NT, preferred_element_type=F32)
                m = jnp.maximum(m, jnp.max(s_p, axis=-1, keepdims=True))
            p_d = jnp.exp2(s_d - m)
            l = jnp.sum(p_d, axis=-1, keepdims=True)
            acc = jnp.dot(p_d.astype(BF16), v_ref[rq, pc], preferred_element_type=F32)
            if past:
                p_p = jnp.exp2(s_p - m)
                l = l + jnp.sum(p_p, axis=-1, keepdims=True)
                acc = acc + jnp.dot(p_p.astype(BF16), v_ref[0:past, pc], preferred_element_type=F32)
            outs.append(acc / l)
        o_ref[rq, pc] = jnp.where(head0, outs[0], outs[1]).astype(o_ref.dtype)


def _moba(proj, q_gain, k_gain, tabs, batch, seq):
    n = proj.shape[0]
    assert seq % MOBA_BLOCK == 0
    n_blocks = seq // MOBA_BLOCK
    assert n_blocks <= 8

    def col(off):
        return pl.BlockSpec((seq, MOBA_W), lambda b, off=off: (b, off // MOBA_W))

    tab = pl.BlockSpec((seq, LANES), lambda b: (0, 0))
    vec = pl.BlockSpec((1, LANES), lambda b: (0, 0))
    return pl.pallas_call(
        functools.partial(_moba_kernel, n_blocks=n_blocks),
        grid=(batch,),
        in_specs=[col(OFF_MQ), col(OFF_MK), col(OFF_MV), vec, vec, tab, tab],
        out_specs=pl.BlockSpec((seq, MOBA_W), lambda b: (b, 0)),
        out_shape=jax.ShapeDtypeStruct((n, MOBA_W), BF16),
        scratch_shapes=[pltpu.VMEM((seq, LANES), F32), pltpu.VMEM((seq, LANES), BF16),
                        pltpu.VMEM((seq, LANES), BF16), pltpu.VMEM((seq, LANES), BF16),
                        pltpu.VMEM((seq, LANES), BF16), pltpu.VMEM((LANES, LANES), F32)],
        compiler_params=_cparams("parallel"),
        name="moba",
    )(proj, proj, proj, jnp.tile(q_gain, 2).reshape(1, LANES), jnp.tile(k_gain, 2).reshape(1, LANES),
      tabs["moba_cos"], tabs["moba_sin"])


def _merge_kernel(x_ref, ga_ref, gb_ref, yr_ref, yc_ref, ym_ref, wr_ref, wc_ref, wm_ref, wo_ref, gf_ref,
                  *rest, with_router):
    if with_router:
        wcat_ref, xo_ref, h_ref, lg_ref = rest
    else:
        xo_ref, h_ref = rest
    d = D_MODEL
    ga = ga_ref[...].astype(F32)
    gb = gb_ref[...].astype(F32)
    g_ret = ga[:, :d]
    g_conv = jnp.concatenate([ga[:, d:], gb[:, :d // 2]], axis=-1)
    g_moba = gb[:, d // 2:]
    merged = (_sigmoid(g_ret) * jnp.dot(yr_ref[...], wr_ref[0], preferred_element_type=F32)
              + _sigmoid(g_conv) * jnp.dot(yc_ref[...], wc_ref[0], preferred_element_type=F32)
              + _sigmoid(g_moba) * jnp.dot(ym_ref[...], wm_ref[0], preferred_element_type=F32))
    x = x_ref[...] + jnp.dot(merged.astype(BF16), wo_ref[0], preferred_element_type=F32)
    xo_ref[...] = x
    h = _rms_norm(x, gf_ref[...])
    if with_router:
        h_ref[...] = _pack_bf16_pairs(h)
        h_hi = h.astype(BF16)
        h_lo = (h - h_hi.astype(F32)).astype(BF16)
        both = jnp.dot(h_hi, wcat_ref[...], preferred_element_type=F32)
        lg_ref[...] = (both[:, :LANES] + both[:, LANES:]
                       + jnp.dot(h_lo, wcat_ref[:, :LANES], preferred_element_type=F32))
    else:
        h_ref[...] = h.astype(BF16)


def _merge(xf, proj, yr, yc, ym, w_ret_o, w_conv_o, w_moba_o, w_out, g_ffn, w_router, layer):
    n, d = xf.shape
    tm = 512
    gw = 3 * d // 2
    with_router = w_router is not None
    row = lambda w: pl.BlockSpec((tm, w), lambda i: (i, 0))
    full = lambda a: pl.BlockSpec(a.shape, lambda i: (0,) * a.ndim)
    of_layer = lambda a: pl.BlockSpec((1,) + a.shape[1:], lambda i: (layer,) + (0,) * (a.ndim - 1))
    gf = g_ffn.reshape(1, d)
    args = [xf, proj, proj, yr, yc, ym, w_ret_o, w_conv_o, w_moba_o, w_out, gf]
    in_specs = [row(d),
                pl.BlockSpec((tm, gw), lambda i: (i, OFF_GATES // gw)),
                pl.BlockSpec((tm, gw), lambda i: (i, OFF_GATES // gw + 1)),
                row(RET_W), row(CONV_CH), row(MOBA_W), of_layer(w_ret_o), of_layer(w_conv_o),
                of_layer(w_moba_o), of_layer(w_out), full(gf)]
    if with_router:
        out_shape = [jax.ShapeDtypeStruct((n, d), F32), jax.ShapeDtypeStruct((n, d // 2), jnp.uint32)]
        out_specs = [row(d), row(d // 2)]
    else:
        out_shape = [jax.ShapeDtypeStruct((n, d), F32), jax.ShapeDtypeStruct((n, d), BF16)]
        out_specs = [row(d), row(d)]
    if with_router:
        wrt = jnp.zeros((d, LANES), F32).at[:, :N_EXPERTS].set(w_router)
        w_hi = wrt.astype(BF16)
        w_lo = (wrt - w_hi.astype(F32)).astype(BF16)
        w_cat = jnp.concatenate([w_hi, w_lo], axis=1)
        args.append(w_cat)
        in_specs.append(full(w_cat))
        out_shape.append(jax.ShapeDtypeStruct((n, LANES), F32))
        out_specs.append(row(LANES))
    return pl.pallas_call(
        functools.partial(_merge_kernel, with_router=with_router),
        grid=(n // tm,),
        in_specs=in_specs, out_specs=out_specs, out_shape=out_shape,
        compiler_params=_cparams("parallel"),
        name="merge",
    )(*args)


def _ffn_kernel(x_ref, h_ref, wg_ref, wu_ref, wd_ref, gn_ref, o_ref, hn_ref):
    f = pl.program_id(1)
    h = h_ref[...]
    g = jnp.dot(h, wg_ref[0], preferred_element_type=F32)
    u = jnp.dot(h, wu_ref[0], preferred_element_type=F32)
    a = (g * _sigmoid(g) * u).astype(BF16)
    dlt = jnp.dot(a, wd_ref[0], preferred_element_type=F32)

    @pl.when(f == 0)
    def _():
        o_ref[...] = x_ref[...] + dlt

    @pl.when(f > 0)
    def _():
        o_ref[...] += dlt

    @pl.when(f == pl.num_programs(1) - 1)
    def _():
        hn_ref[...] = _rms_norm(o_ref[...], gn_ref[...]).astype(hn_ref.dtype)


def _dense_ffn(xf, h, w_gate, w_up, w_down, layer, next_gain):
    n, d = xf.shape
    ff = w_gate.shape[2]
    tm, tf = 512, 1408
    assert ff % tf == 0
    return pl.pallas_call(
        _ffn_kernel,
        grid=(n // tm, ff // tf),
        in_specs=[pl.BlockSpec((tm, d), lambda i, f: (i, 0)),
                  pl.BlockSpec((tm, d), lambda i, f: (i, 0)),
                  pl.BlockSpec((1, d, tf), lambda i, f: (layer, 0, f)),
                  pl.BlockSpec((1, d, tf), lambda i, f: (layer, 0, f)),
                  pl.BlockSpec((1, tf, d), lambda i, f: (layer, f, 0)),
                  pl.BlockSpec((1, d), lambda i, f: (0, 0))],
        out_specs=[pl.BlockSpec((tm, d), lambda i, f: (i, 0)), pl.BlockSpec((tm, d), lambda i, f: (i, 0))],
        out_shape=[jax.ShapeDtypeStruct((n, d), F32), jax.ShapeDtypeStruct((n, d), BF16)],
        compiler_params=_cparams("parallel", "arbitrary"),
        name="dense_ffn",
    )(xf, h, w_gate, w_up, w_down, next_gain.reshape(1, d))


def _row_copies(n_groups, group_copies):
    def start(g, carry):
        for cp in group_copies(g):
            cp.start()
        return carry

    def wait(g, carry):
        for cp in group_copies(0):
            cp.wait()
        return carry

    lax.fori_loop(0, n_groups, start, 0, unroll=2)
    lax.fori_loop(0, n_groups, wait, 0, unroll=2)


def _dispatch_kernel(dest_ref, hp_ref, xz_ref, xbuf_ref, sem, *, n_tok):
    del xz_ref

    def group(g):
        tok0 = pl.multiple_of(g * SUBLANES, SUBLANES)
        return [pltpu.make_async_copy(hp_ref.at[pl.ds(tok0 + r, 1), :],
                                      xbuf_ref.at[pl.ds(dest_ref[0, 0, TOP_K * (tok0 + r) + k], 1), :], sem)
                for r in range(SUBLANES) for k in range(TOP_K)]

    _row_copies(n_tok // SUBLANES, group)


def _dispatch(hp, dest3, p_rows):
    n, w = hp.shape
    steps, _, per_step = dest3.shape
    n_tok = per_step // TOP_K
    return pl.pallas_call(
        functools.partial(_dispatch_kernel, n_tok=n_tok),
        grid=(steps,),
        in_specs=[pl.BlockSpec((1, 1, per_step), lambda i: (i, 0, 0), memory_space=pltpu.SMEM),
                  pl.BlockSpec((n_tok, w), lambda i: (i, 0)),
                  pl.BlockSpec(memory_space=pl.ANY)],
        out_specs=pl.BlockSpec(memory_space=pl.ANY),
        out_shape=jax.ShapeDtypeStruct((p_rows, w), hp.dtype),
        scratch_shapes=[pltpu.SemaphoreType.DMA(())],
        input_output_aliases={2: 0},
        compiler_params=_cparams("arbitrary"),
        name="moe_dispatch",
    )(dest3, hp, jnp.zeros((p_rows, w), hp.dtype))


def _moe_kernel(te_ref, act_ref, x_ref, wg_ref, wu_ref, wd_ref, o_ref, xs_ref):
    i = pl.program_id(0)
    f = pl.program_id(1)
    active = act_ref[i] > 0
    half = xs_ref.shape[1] // 2

    @pl.when(active & (f == 0))
    def _():
        left, right = _unpack_bf16_pairs(x_ref[...])
        xs_ref[:, :half] = left
        xs_ref[:, half:] = right

    @pl.when(active)
    def _():
        x = xs_ref[...]
        g = jnp.dot(x, wg_ref[0, 0], preferred_element_type=F32)
        u = jnp.dot(x, wu_ref[0, 0], preferred_element_type=F32)
        a = (g * _sigmoid(g) * u).astype(BF16)
        dlt = jnp.dot(a, wd_ref[0, 0], preferred_element_type=F32)

        @pl.when(f == 0)
        def _():
            o_ref[...] = dlt

        @pl.when(f > 0)
        def _():
            o_ref[...] += dlt

    @pl.when(jnp.logical_not(active) & (f == 0))
    def _():
        o_ref[...] = jnp.zeros(o_ref.shape, o_ref.dtype)


def _moe_experts(xbuf, tile_expert, tile_active, w_gate, w_up, w_down, layer, tm):
    p, half = xbuf.shape
    d = 2 * half
    ff = w_gate.shape[3]
    tf = 1792
    assert ff % tf == 0
    nf = ff // tf

    def fsel(f, act, i):
        return jnp.where(act[i] > 0, f, nf - 1)

    grid_spec = pltpu.PrefetchScalarGridSpec(
        num_scalar_prefetch=2,
        grid=(p // tm, nf),
        in_specs=[pl.BlockSpec((tm, half), lambda i, f, te, act: (i, 0)),
                  pl.BlockSpec((1, 1, d, tf), lambda i, f, te, act: (layer, te[i], 0, fsel(f, act, i))),
                  pl.BlockSpec((1, 1, d, tf), lambda i, f, te, act: (layer, te[i], 0, fsel(f, act, i))),
                  pl.BlockSpec((1, 1, tf, d), lambda i, f, te, act: (layer, te[i], fsel(f, act, i), 0))],
        out_specs=pl.BlockSpec((tm, d), lambda i, f, te, act: (i, 0)),
        scratch_shapes=[pltpu.VMEM((tm, d), BF16)],
    )
    return pl.pallas_call(
        _moe_kernel,
        grid_spec=grid_spec,
        out_shape=jax.ShapeDtypeStruct((p, d), F32),
        compiler_params=_cparams("arbitrary", "arbitrary"),
        name="moe_experts",
    )(tile_expert, tile_active, xbuf, w_gate, w_up, w_down)


def _combine_kernel(dest_ref, x_ref, g_ref, gn_ref, ybuf_ref, o_ref, hn_ref, ys_ref, sem, *, n_tok):
    def group(g):
        tok0 = pl.multiple_of(g * SUBLANES, SUBLANES)
        return [pltpu.make_async_copy(ybuf_ref.at[pl.ds(dest_ref[0, 0, TOP_K * (tok0 + r) + k], 1), :],
                                      ys_ref.at[pl.ds(k * n_tok + tok0 + r, 1), :], sem)
                for r in range(SUBLANES) for k in range(TOP_K)]

    _row_copies(n_tok // SUBLANES, group)
    g = g_ref[...]
    y = g[:, 0:1] * ys_ref[0:n_tok, :] + g[:, 1:2] * ys_ref[n_tok:, :]
    out = x_ref[...] + y
    o_ref[...] = out
    hn_ref[...] = _rms_norm(out, gn_ref[...]).astype(hn_ref.dtype)


def _combine(xf, gates, dest3, ybuf, next_gain):
    n, d = xf.shape
    steps, _, per_step = dest3.shape
    n_tok = per_step // TOP_K
    assert TOP_K == 2
    row = pl.BlockSpec((n_tok, d), lambda i: (i, 0))
    return pl.pallas_call(
        functools.partial(_combine_kernel, n_tok=n_tok),
        grid=(steps,),
        in_specs=[pl.BlockSpec((1, 1, per_step), lambda i: (i, 0, 0), memory_space=pltpu.SMEM),
                  row,
                  pl.BlockSpec((n_tok, TOP_K), lambda i: (i, 0)),
                  pl.BlockSpec((1, d), lambda i: (0, 0)),
                  pl.BlockSpec(memory_space=pl.ANY)],
        out_specs=[row, row],
        out_shape=[jax.ShapeDtypeStruct((n, d), F32), jax.ShapeDtypeStruct((n, d), BF16)],
        scratch_shapes=[pltpu.VMEM((TOP_K * n_tok, d), F32), pltpu.SemaphoreType.DMA(())],
        compiler_params=_cparams("arbitrary"),
        name="moe_combine",
    )(dest3, xf, gates, next_gain.reshape(1, d), ybuf)


def _moe(xf, hp, logits, w_gate, w_up, w_down, layer, next_gain):
    n, d = xf.shape
    na = n * TOP_K
    tm = 512
    n_tok = 512
    n_tiles = na // tm + N_EXPERTS
    top_vals, top_idx = lax.top_k(logits, TOP_K)
    gates = jax.nn.softmax(top_vals, axis=-1)
    e_flat = top_idx.reshape(-1)
    onehot = (e_flat[:, None] == jnp.arange(N_EXPERTS, dtype=e_flat.dtype)[None, :]).astype(jnp.int32)
    csum = jnp.cumsum(onehot, axis=0)
    pos_in_e = jnp.sum(csum * onehot, axis=1) - 1
    counts = csum[-1]
    tiles_per_e = (counts + tm - 1) // tm
    tile_end = jnp.cumsum(tiles_per_e)
    tile_start = tile_end - tiles_per_e
    dest = jnp.sum(tile_start[None, :] * onehot, axis=1) * tm + pos_in_e
    tile_ids = jnp.arange(n_tiles, dtype=jnp.int32)
    tile_expert = jnp.minimum(jnp.sum((tile_end[None, :] <= tile_ids[:, None]).astype(jnp.int32), axis=1),
                              N_EXPERTS - 1).astype(jnp.int32)
    tile_active = (tile_ids < tile_end[-1]).astype(jnp.int32)
    dest3 = dest.astype(jnp.int32).reshape(n // n_tok, 1, TOP_K * n_tok)
    xbuf = _dispatch(hp, dest3, n_tiles * tm)
    ybuf = _moe_experts(xbuf, tile_expert, tile_active, w_gate, w_up, w_down, layer, tm)
    return _combine(xf, gates, dest3, ybuf, next_gain)


def _rope(seq, dim):
    inv = 1.0 / (ROPE_THETA ** (jnp.arange(0, dim, 2, dtype=F32) / dim))
    ang = jnp.arange(seq, dtype=F32)[:, None] * inv[None, :]
    return jnp.cos(ang), jnp.sin(ang)


def _tables(seq):
    cos_r, sin_r = _rope(seq, RET_HEAD_DIM)
    cos_m, sin_m = _rope(seq, MOBA_HEAD_DIM)
    c_len = RET_CHUNK
    log_g = jnp.log(1.0 - jnp.exp2(-5.0 - jnp.arange(RET_HEADS, dtype=F32)))
    pos = jnp.arange(c_len, dtype=F32)
    diff = pos[:, None] - pos[None, :]
    decay = jnp.where(diff[None] >= 0, jnp.exp(diff[None] * log_g[:, None, None]), 0.0)
    w_state = jnp.exp((c_len - 1.0 - pos)[None, :] * log_g[:, None])
    w_cross = jnp.exp((pos + 1.0)[None, :] * log_g[:, None])
    g_chunk = jnp.exp(c_len * log_g)
    bcast = lambda a: jnp.broadcast_to(a[:, :, None], (RET_HEADS, c_len, RET_HEAD_DIM))
    return {
        "ret_cos": jnp.concatenate([cos_r, cos_r], axis=-1),
        "ret_sin": jnp.concatenate([-sin_r, sin_r], axis=-1),
        "ret_decay": decay,
        "ret_wstate": bcast(w_state),
        "ret_wcross": bcast(w_cross),
        "ret_gchunk": jnp.broadcast_to(g_chunk[:, None, None], (RET_HEADS, RET_HEAD_DIM, RET_HEAD_DIM)),
        "moba_cos": jnp.tile(cos_m, (1, 2 * LANES // MOBA_HEAD_DIM)),
        "moba_sin": jnp.tile(jnp.concatenate([-sin_m, sin_m], axis=-1), (1, LANES // MOBA_HEAD_DIM)),
    }


def kernel(x, g_mix, w_in, conv_w, conv_b, conv_ln_g, conv_ln_b, q_norm_g, k_norm_g, w_ret_o, w_conv_o,
           w_moba_o, w_out, g_ffn, w_ff_gate, w_ff_up, w_ff_down, w_router, w_e_gate, w_e_up, w_e_down):
    batch, seq, d = x.shape
    depth = g_mix.shape[0]
    assert d == D_MODEL and w_in.shape[2] == IN_COLS
    xf = x.reshape(batch * seq, d)
    tabs = _tables(seq)
    w_ret_o, w_conv_o, w_moba_o, w_out, w_ff_gate, w_ff_up, w_ff_down, w_e_gate, w_e_up, w_e_down = (
        w.astype(BF16) for w in (w_ret_o, w_conv_o, w_moba_o, w_out, w_ff_gate, w_ff_up, w_ff_down,
                                 w_e_gate, w_e_up, w_e_down))
    hn = _rmsnorm(xf, g_mix[0])
    for l in range(depth):
        next_gain = g_mix[l + 1] if l + 1 < depth else jnp.ones((d,), F32)
        proj = _inproj(hn, w_in, l)
        yr = _retention(proj, tabs, batch, seq)
        yc = _conv_branch(proj, conv_w[l], conv_b[l], conv_ln_g[l], conv_ln_b[l], batch, seq)
        ym = _moba(proj, q_norm_g[l], k_norm_g[l], tabs, batch, seq)
        j = l // 2
        if l % 2 == 0:
            xf, h = _merge(xf, proj, yr, yc, ym, w_ret_o, w_conv_o, w_moba_o, w_out, g_ffn[l], None, l)
            xf, hn = _dense_ffn(xf, h, w_ff_gate, w_ff_up, w_ff_down, j, next_gain)
        else:
            xf, hp, logits = _merge(xf, proj, yr, yc, ym, w_ret_o, w_conv_o, w_moba_o, w_out, g_ffn[l],
                                    w_router[j], l)
            xf, hn = _moe(xf, hp, logits[:, :N_EXPERTS], w_e_gate, w_e_up, w_e_down, j, next_gain)
    return xf.reshape(batch, seq, d)
```
